```python
import jax, jax.numpy as jnp
from jax import lax
import numpy as np

D_MODEL = 2048
BATCH = 16
SEQ = 2048
DEPTH = 2

GRID_W = 64
Q_BLOCK = 128
ROPE_THETA = 10000.0
EPS = 1e-6

GQA_HEADS = 6
GQA_KV_HEADS = 2
GQA_HEAD_DIM = 128
GQA_WIDTH = GQA_HEADS * GQA_HEAD_DIM
GQA_KV_WIDTH = GQA_KV_HEADS * GQA_HEAD_DIM

MLA_HEADS = 4
MLA_Q_LORA = 512
MLA_KV_LORA = 256
MLA_NOPE_DIM = 128
MLA_ROPE_DIM = 64
MLA_V_DIM = 128
MLA_WIDTH = MLA_HEADS * MLA_V_DIM

SSD_HEADS = 12
SSD_HEAD_DIM = 64
SSD_GROUPS = 2
SSD_STATE = 128
SSD_CONV = 5
SSD_CHUNK = 128
SSD_INNER = SSD_HEADS * SSD_HEAD_DIM
SSD_CONV_DIM = SSD_INNER + 2 * SSD_GROUPS * SSD_STATE

MIX_WIDTH = GQA_WIDTH + MLA_WIDTH + SSD_INNER
IN_SPLITS = (GQA_WIDTH, GQA_KV_WIDTH, GQA_KV_WIDTH, MLA_Q_LORA, MLA_KV_LORA, MLA_ROPE_DIM, SSD_INNER, SSD_CONV_DIM, 2 * SSD_HEADS)
IN_COLS = GQA_WIDTH + 2 * GQA_KV_WIDTH + MLA_Q_LORA + MLA_KV_LORA + MLA_ROPE_DIM + SSD_INNER + SSD_CONV_DIM + 2 * SSD_HEADS

FFN_HIDDEN = -(-8 * D_MODEL // (3 * 256)) * 256

kernel_name = 'hybrid_gqa_mla_ssd_encoder_block'


def rms_norm(x, g):
    xf = x.astype(jnp.float32)
    y = xf * lax.rsqrt(jnp.mean(xf * xf, axis=-1, keepdims=True) + EPS)
    return (y * g).astype(x.dtype)


def axial_rope_tables(seq_len, rot_dim, dtype):
    rows = seq_len // GRID_W
    row_idx = jnp.repeat(jnp.arange(rows), GRID_W).astype(jnp.float32)
    col_idx = jnp.tile(jnp.arange(GRID_W), rows).astype(jnp.float32)
    axis_dim = rot_dim // 2
    inv_freq = jnp.power(ROPE_THETA, -jnp.arange(0, axis_dim, 2, dtype=jnp.float32) / axis_dim)
    ang_r = row_idx[:, None] * inv_freq[None, :]
    ang_c = col_idx[:, None] * inv_freq[None, :]
    return (jnp.cos(ang_r).astype(dtype), jnp.sin(ang_r).astype(dtype),
            jnp.cos(ang_c).astype(dtype), jnp.sin(ang_c).astype(dtype))


def rotate(x, cos, sin):
    x1, x2 = jnp.split(x, 2, axis=-1)
    cos = cos[:, None, :]
    sin = sin[:, None, :]
    return jnp.concatenate([x1 * cos - x2 * sin, x1 * sin + x2 * cos], axis=-1)


def apply_axial_rope(x, tables):
    cos_r, sin_r, cos_c, sin_c = tables
    x_row, x_col = jnp.split(x, 2, axis=-1)
    return jnp.concatenate([rotate(x_row, cos_r, sin_r), rotate(x_col, cos_c, sin_c)], axis=-1)


def blocked_attention(q, k, v, scale):
    b, s, h, dk = q.shape
    hkv, dv = k.shape[2], v.shape[-1]
    rep = h // hkv
    nb = s // Q_BLOCK
    qb = q.reshape(b, nb, Q_BLOCK, hkv, rep, dk).transpose(1, 0, 2, 3, 4, 5)

    def attend(q_blk):
        logits = jnp.einsum('bqgrd,bsgd->bgrqs', q_blk, k).astype(jnp.float32) * scale
        probs = jax.nn.softmax(logits, axis=-1).astype(v.dtype)
        return jnp.einsum('bgrqs,bsgd->bqgrd', probs, v)

    out = lax.map(attend, qb)
    return out.transpose(1, 0, 2, 3, 4, 5).reshape(b, s, h * dv)


def segsum(a):
    t = a.shape[-1]
    a_rep = jnp.broadcast_to(a[..., :, None], a.shape + (t,))
    strict_lower = jnp.tril(jnp.ones((t, t), dtype=bool), k=-1)
    seg = jnp.cumsum(jnp.where(strict_lower, a_rep, 0.0), axis=-2)
    lower = jnp.tril(jnp.ones((t, t), dtype=bool))
    return jnp.where(lower, seg, -jnp.inf)


def ssd_chunked(x, dt, a_neg, bm, cm):
    b, s, h, p = x.shape
    g, n = bm.shape[2], bm.shape[3]
    e = h // g
    nc = s // SSD_CHUNK
    f32 = jnp.float32
    xd = (x.astype(f32) * dt[..., None]).reshape(b, nc, SSD_CHUNK, g, e, p)
    a = (dt * a_neg).reshape(b, nc, SSD_CHUNK, g, e).transpose(0, 3, 4, 1, 2)
    bc = bm.astype(f32).reshape(b, nc, SSD_CHUNK, g, n)
    cc = cm.astype(f32).reshape(b, nc, SSD_CHUNK, g, n)
    a_cs = jnp.cumsum(a, axis=-1)
    cb = jnp.einsum('bclgn,bcsgn->bgcls', cc, bc)
    w_diag = cb[:, :, None] * jnp.exp(segsum(a))
    y_diag = jnp.einsum('bgecls,bcsgep->bclgep', w_diag, xd)
    to_end = jnp.exp(a_cs[..., -1:] - a_cs).transpose(0, 3, 4, 1, 2)
    states = jnp.einsum('bclgn,bclgep->bcgepn', bc, xd * to_end[..., None])
    states = jnp.concatenate([jnp.zeros_like(states[:, :1]), states], axis=1)
    chunk_a = jnp.pad(a_cs[..., -1], ((0, 0), (0, 0), (0, 0), (1, 0)))
    chunk_decay = jnp.exp(segsum(chunk_a))
    states = jnp.einsum('bgezc,bcgepn->bzgepn', chunk_decay, states)[:, :-1]
    from_start = jnp.exp(a_cs).transpose(0, 3, 4, 1, 2)
    y_off = jnp.einsum('bclgn,bcgepn->bclgep', cc, states) * from_start[..., None]
    return (y_diag + y_off).reshape(b, s, h, p)


def depthwise_centred_conv(x, w):
    pad = SSD_CONV // 2
    return lax.conv_general_dilated(x, w[:, None, :], window_strides=(1,), padding=[(pad, pad)],
                                    dimension_numbers=('NWC', 'WIO', 'NWC'), feature_group_count=x.shape[-1])


def gqa_group(q, k, v, q_norm_g, k_norm_g, rope):
    b, s = q.shape[:2]
    q = q.reshape(b, s, GQA_HEADS, GQA_HEAD_DIM)
    k = k.reshape(b, s, GQA_KV_HEADS, GQA_HEAD_DIM)
    v = v.reshape(b, s, GQA_KV_HEADS, GQA_HEAD_DIM)
    q = apply_axial_rope(rms_norm(q, q_norm_g), rope)
    k = apply_axial_rope(rms_norm(k, k_norm_g), rope)
    return blocked_attention(q, k, v, GQA_HEAD_DIM ** -0.5)


def mla_group(c_q, c_kv, k_pe, q_norm_g, w_uq, kv_norm_g, w_ukv, rope):
    b, s = c_q.shape[:2]
    q = (rms_norm(c_q, q_norm_g) @ w_uq).reshape(b, s, MLA_HEADS, MLA_NOPE_DIM + MLA_ROPE_DIM)
    q_nope, q_pe = q[..., :MLA_NOPE_DIM], q[..., MLA_NOPE_DIM:]
    kv = (rms_norm(c_kv, kv_norm_g) @ w_ukv).reshape(b, s, MLA_HEADS, MLA_NOPE_DIM + MLA_V_DIM)
    k_nope, v = kv[..., :MLA_NOPE_DIM], kv[..., MLA_NOPE_DIM:]
    q_pe = apply_axial_rope(q_pe, rope)
    k_pe = apply_axial_rope(k_pe[:, :, None, :], rope)
    q = jnp.concatenate([q_nope, q_pe], axis=-1)
    k = jnp.concatenate([k_nope, jnp.broadcast_to(k_pe, (b, s, MLA_HEADS, MLA_ROPE_DIM))], axis=-1)
    return blocked_attention(q, k, v, (MLA_NOPE_DIM + MLA_ROPE_DIM) ** -0.5)


def ssd_group(z, xbc, dt_raw, conv_w, conv_b, dt_bias, a_log, d_skip, norm_g):
    b, s = z.shape[:2]
    xbc = jax.nn.silu(depthwise_centred_conv(xbc, conv_w) + conv_b)
    xs, bm, cm = jnp.split(xbc, [SSD_INNER, SSD_INNER + SSD_GROUPS * SSD_STATE], axis=-1)
    xs = xs.reshape(b, s, SSD_HEADS, SSD_HEAD_DIM)
    bm = bm.reshape(b, s, SSD_GROUPS, SSD_STATE)
    cm = cm.reshape(b, s, SSD_GROUPS, SSD_STATE)
    dt = jax.nn.softplus(dt_raw.reshape(b, s, 2, SSD_HEADS).astype(jnp.float32) + dt_bias)
    a_neg = -jnp.exp(a_log.astype(jnp.float32))
    y_fwd = ssd_chunked(xs, dt[:, :, 0], a_neg[0], bm, cm)
    flip = lambda t: jnp.flip(t, axis=1)
    y_bwd = flip(ssd_chunked(flip(xs), flip(dt[:, :, 1]), a_neg[1], flip(bm), flip(cm)))
    y = y_fwd + y_bwd + xs * d_skip[:, None]
    y = y.reshape(b, s, SSD_INNER) * jax.nn.silu(z)
    y = rms_norm(y.reshape(b, s, SSD_GROUPS, SSD_INNER // SSD_GROUPS), norm_g.reshape(SSD_GROUPS, -1))
    return y.reshape(b, s, SSD_INNER)


def hybrid_mixer(h, w_in, q_norm_g, k_norm_g, mla_q_norm_g, w_uq, mla_kv_norm_g, w_ukv,
                 conv_w, conv_b, dt_bias, a_log, d_skip, ssd_norm_g, w_out, rope_a, rope_b):
    proj = h @ w_in
    idx = np.cumsum(IN_SPLITS)[:-1].tolist()
    q_a, k_a, v_a, cq_b, ckv_b, kpe_b, z_c, xbc_c, dt_c = jnp.split(proj, idx, axis=-1)
    o_a = gqa_group(q_a, k_a, v_a, q_norm_g, k_norm_g, rope_a)
    o_b = mla_group(cq_b, ckv_b, kpe_b, mla_q_norm_g, w_uq, mla_kv_norm_g, w_ukv, rope_b)
    o_c = ssd_group(z_c, xbc_c, dt_c, conv_w, conv_b, dt_bias, a_log, d_skip, ssd_norm_g)
    o = jnp.concatenate([o_a.astype(h.dtype), o_b.astype(h.dtype), o_c.astype(h.dtype)], axis=-1)
    return o @ w_out


def swiglu_ffn(h, w_gate_up, w_down):
    gate, up = jnp.split(h @ w_gate_up, 2, axis=-1)
    return (jax.nn.silu(gate) * up) @ w_down


def setup_inputs(seed: int = 0) -> dict:
    key = jax.random.key(seed)
    ks = iter(jax.random.split(key, 32))
    f32 = jnp.float32
    D, L = D_MODEL, DEPTH

    def nrm(shape, std):
        return std * jax.random.normal(next(ks), shape, f32)

    def gain(shape):
        return 1.0 + nrm(shape, 0.02)

    dt_init = jnp.exp(jax.random.uniform(next(ks), (L, 2, SSD_HEADS), f32, np.log(1e-3), np.log(1e-1)))
    dt_bias = dt_init + jnp.log(-jnp.expm1(-dt_init))
    a_log = jnp.log(jax.random.uniform(next(ks), (L, 2, SSD_HEADS), f32, 1.0, 16.0))
    return {
        'x': nrm((BATCH, SEQ, D), 1.0),
        'c': nrm((BATCH, D), 1.0),
        'w_ada': nrm((L, D, 6 * D), 0.5 * D ** -0.5),
        'b_ada': nrm((L, 6 * D), 0.01),
        'norm1_g': gain((L, D)),
        'norm2_g': gain((L, D)),
        'w_in': nrm((L, D, IN_COLS), D ** -0.5),
        'q_norm_g': gain((L, GQA_HEAD_DIM)),
        'k_norm_g': gain((L, GQA_HEAD_DIM)),
        'mla_q_norm_g': gain((L, MLA_Q_LORA)),
        'w_uq': nrm((L, MLA_Q_LORA, MLA_HEADS * (MLA_NOPE_DIM + MLA_ROPE_DIM)), MLA_Q_LORA ** -0.5),
        'mla_kv_norm_g': gain((L, MLA_KV_LORA)),
        'w_ukv': nrm((L, MLA_KV_LORA, MLA_HEADS * (MLA_NOPE_DIM + MLA_V_DIM)), MLA_KV_LORA ** -0.5),
        'conv_w': nrm((L, SSD_CONV, SSD_CONV_DIM), SSD_CONV ** -0.5),
        'conv_b': nrm((L, SSD_CONV_DIM), 0.01),
        'dt_bias': dt_bias,
        'a_log': a_log,
        'd_skip': gain((L, SSD_HEADS)),
        'ssd_norm_g': gain((L, SSD_INNER)),
        'w_out': nrm((L, MIX_WIDTH, D), MIX_WIDTH ** -0.5),
        'w_gate_up': nrm((L, D, 2 * FFN_HIDDEN), D ** -0.5),
        'w_down': nrm((L, FFN_HIDDEN, D), FFN_HIDDEN ** -0.5),
        'final_norm_g': gain((D,)),
    }


def reference(x, c, w_ada, b_ada, norm1_g, norm2_g, w_in, q_norm_g, k_norm_g, mla_q_norm_g, w_uq,
              mla_kv_norm_g, w_ukv, conv_w, conv_b, dt_bias, a_log, d_skip, ssd_norm_g, w_out,
              w_gate_up, w_down, final_norm_g):
    seq_len = x.shape[1]
    rope_a = axial_rope_tables(seq_len, GQA_HEAD_DIM, x.dtype)
    rope_b = axial_rope_tables(seq_len, MLA_ROPE_DIM, x.dtype)
    c_act = jax.nn.silu(c)
    for l in range(DEPTH):
        mod = c_act @ w_ada[l] + b_ada[l]
        shift1, scale1, gate1, shift2, scale2, gate2 = [m[:, None, :] for m in jnp.split(mod, 6, axis=-1)]
        h = rms_norm(x, norm1_g[l]) * (1 + scale1) + shift1
        mix = hybrid_mixer(h, w_in[l], q_norm_g[l], k_norm_g[l], mla_q_norm_g[l], w_uq[l], mla_kv_norm_g[l],
                           w_ukv[l], conv_w[l], conv_b[l], dt_bias[l], a_log[l], d_skip[l], ssd_norm_g[l],
                           w_out[l], rope_a, rope_b)
        x = x + gate1 * mix
        h = rms_norm(x, norm2_g[l]) * (1 + scale2) + shift2
        x = x + gate2 * swiglu_ffn(h, w_gate_up[l], w_down[l])
    return rms_norm(x, final_norm_g)
```

```python
import functools

import numpy as np
import jax
import jax.numpy as jnp
from jax import lax
from jax.experimental import pallas as pl
from jax.experimental.pallas import tpu as pltpu

F32 = jnp.float32
BF16 = jnp.bfloat16

GRID_W = 64
ROPE_THETA = 10000.0
EPS = 1e-6

GQA_HEADS = 6
GQA_KV_HEADS = 2
GQA_HEAD_DIM = 128
GQA_WIDTH = GQA_HEADS * GQA_HEAD_DIM
GQA_KV_WIDTH = GQA_KV_HEADS * GQA_HEAD_DIM

MLA_HEADS = 4
MLA_Q_LORA = 512
MLA_KV_LORA = 256
MLA_NOPE_DIM = 128
MLA_ROPE_DIM = 64
MLA_V_DIM = 128
MLA_QK_DIM = MLA_NOPE_DIM + MLA_ROPE_DIM
MLA_QK_PAD = 256
MLA_WIDTH = MLA_HEADS * MLA_V_DIM

SSD_HEADS = 12
SSD_HEAD_DIM = 64
SSD_GROUPS = 2
SSD_STATE = 128
SSD_CONV = 5
SSD_CHUNK = 128
SSD_INNER = SSD_HEADS * SSD_HEAD_DIM
SSD_CONV_DIM = SSD_INNER + 2 * SSD_GROUPS * SSD_STATE
SSD_GHEADS = SSD_HEADS // SSD_GROUPS
SSD_GINNER = SSD_INNER // SSD_GROUPS
SSD_GCONV = SSD_GINNER + 2 * SSD_STATE

LANES = 128
CONV_PAD_ROWS = 8

SEG_XBC = (0, SSD_GROUPS * SSD_GCONV)
SEG_Z = (SEG_XBC[1], SEG_XBC[1] + SSD_INNER)
SEG_DT = (SEG_Z[1], SEG_Z[1] + SSD_GROUPS * LANES)
SEG_QA = (SEG_DT[1], SEG_DT[1] + GQA_WIDTH)
SEG_KA = (SEG_QA[1], SEG_QA[1] + GQA_KV_WIDTH)
SEG_VA = (SEG_KA[1], SEG_KA[1] + GQA_KV_WIDTH)
SEG_CQ = (SEG_VA[1], SEG_VA[1] + MLA_Q_LORA)
SEG_CKV = (SEG_CQ[1], SEG_CQ[1] + MLA_KV_LORA)
SEG_KPE = (SEG_CKV[1], SEG_CKV[1] + LANES)
IN_COLS_PAD = SEG_KPE[1]

VMEM_LIMIT = 56 * 1024 * 1024


def _cparams(sem):
    return pltpu.CompilerParams(dimension_semantics=sem, vmem_limit_bytes=VMEM_LIMIT)


def _dot(a, b):
    return jnp.dot(a, b, preferred_element_type=F32)


def _dot_nt(a, b):
    return lax.dot_general(a, b, (((1,), (1,)), ((), ())), preferred_element_type=F32)


def _rms(x, g):
    return x * lax.rsqrt(jnp.mean(x * x, axis=-1, keepdims=True) + EPS) * g


def _silu(x):
    return x * (1.0 / (1.0 + jnp.exp(-x)))


def _swap_quarters(x, q):
    lane = lax.broadcasted_iota(jnp.int32, x.shape, 1)
    even = (lane // q) % 2 == 0
    return jnp.where(even, pltpu.roll(x, LANES - q, 1), pltpu.roll(x, q, 1))


def _rope(x, cos, sin, q):
    return x * cos + _swap_quarters(x, q) * sin


def _mod_kernel(c_ref, w_ref, b_ref, o_ref):
    c = c_ref[...]
    act = _silu(c).astype(BF16)
    o_ref[...] = _dot(act, w_ref[...].astype(BF16)) + b_ref[...]


def _modulation(c, w_ada, b_ada):
    depth, d, n = w_ada.shape
    b = c.shape[0]
    tn = 1024
    return pl.pallas_call(
        _mod_kernel,
        grid=(depth, n // tn),
        in_specs=[
            pl.BlockSpec((b, d), lambda l, j: (0, 0)),
            pl.BlockSpec((None, d, tn), lambda l, j: (l, 0, j)),
            pl.BlockSpec((None, 1, tn), lambda l, j: (l, 0, j)),
        ],
        out_specs=pl.BlockSpec((None, b, tn), lambda l, j: (l, 0, j)),
        out_shape=jax.ShapeDtypeStruct((depth, b, n), F32),
        compiler_params=_cparams(("arbitrary", "arbitrary")),
        name="modulation",
    )(c, w_ada, b_ada.reshape(depth, 1, n))


def _prep_kernel(x_ref, mod_ref, g1_ref, win_ref, qg_ref, kg_ref, cosa_ref, sina_ref,
                 mqg_ref, wuq_ref, mkvg_ref, wukv_ref, cosb_ref, sinb_ref,
                 xbc_ref, z_ref, dt_ref, qa_ref, ka_ref, va_ref, qb_ref, kb_ref, vb_ref):
    x = x_ref[...]
    shift = mod_ref[0:1, :]
    scale = mod_ref[1:2, :]
    h = (_rms(x, g1_ref[...]) * (1.0 + scale) + shift).astype(BF16)

    def proj(seg):
        return _dot(h, win_ref[:, seg[0]:seg[1]])

    xbc_ref[...] = proj(SEG_XBC).astype(BF16)
    z_ref[...] = proj(SEG_Z).astype(BF16)
    dt_ref[...] = proj(SEG_DT)
    va_ref[...] = proj(SEG_VA).astype(BF16)

    cosa = cosa_ref[...]
    sina = sina_ref[...]
    qa = proj(SEG_QA)
    qscale = GQA_HEAD_DIM ** -0.5
    for hd in range(GQA_HEADS):
        sl = slice(hd * GQA_HEAD_DIM, (hd + 1) * GQA_HEAD_DIM)
        q = _rope(_rms(qa[:, sl], qg_ref[...]), cosa, sina, GQA_HEAD_DIM // 4)
        qa_ref[:, sl] = (q * qscale).astype(BF16)
    ka = proj(SEG_KA)
    for hd in range(GQA_KV_HEADS):
        sl = slice(hd * GQA_HEAD_DIM, (hd + 1) * GQA_HEAD_DIM)
        k = _rope(_rms(ka[:, sl], kg_ref[...]), cosa, sina, GQA_HEAD_DIM // 4)
        ka_ref[:, sl] = k.astype(BF16)

    cosb = cosb_ref[...]
    sinb = sinb_ref[...]
    cq = _rms(proj(SEG_CQ), mqg_ref[...]).astype(BF16)
    qb = _dot(cq, wuq_ref[...])
    bscale = MLA_QK_DIM ** -0.5
    for hd in range(MLA_HEADS):
        c0 = hd * MLA_QK_PAD
        qb_ref[:, c0:c0 + LANES] = (qb[:, c0:c0 + LANES] * bscale).astype(BF16)
        qpe = _rope(qb[:, c0 + LANES:c0 + 2 * LANES], cosb, sinb, MLA_ROPE_DIM // 4)
        qb_ref[:, c0 + LANES:c0 + 2 * LANES] = (qpe * bscale).astype(BF16)
    ckv = _rms(proj(SEG_CKV), mkvg_ref[...]).astype(BF16)
    kv = _dot(ckv, wukv_ref[...])
    kpe = _rope(proj(SEG_KPE), cosb, sinb, MLA_ROPE_DIM // 4).astype(BF16)
    for hd in range(MLA_HEADS):
        c0 = hd * MLA_QK_PAD
        kb_ref[:, c0:c0 + LANES] = kv[:, hd * LANES:(hd + 1) * LANES].astype(BF16)
        kb_ref[:, c0 + LANES:c0 + 2 * LANES] = kpe
    vb_ref[...] = kv[:, MLA_HEADS * MLA_NOPE_DIM:].astype(BF16)


def _const_spec(shape):
    nd = len(shape)
    return pl.BlockSpec(shape, lambda *_: (0,) * nd, pipeline_mode=pl.Buffered(1))


def _prep(x, mod, g1, win, qg, kg, cosa, sina, mqg, wuq, mkvg, wukv, cosb, sinb, tm):
    b, s, d = x.shape
    row = lambda w: pl.BlockSpec((None, tm, w), lambda bi, i: (bi, i, 0))
    tab = pl.BlockSpec((tm, LANES), lambda bi, i: (i, 0))
    widths = (SSD_CONV_DIM, SSD_INNER, SSD_GROUPS * LANES, GQA_WIDTH, GQA_KV_WIDTH, GQA_KV_WIDTH,
              MLA_HEADS * MLA_QK_PAD, MLA_HEADS * MLA_QK_PAD, MLA_WIDTH)
    dtypes = (BF16, BF16, F32, BF16, BF16, BF16, BF16, BF16, BF16)
    return pl.pallas_call(
        _prep_kernel,
        grid=(b, s // tm),
        in_specs=[
            row(d),
            pl.BlockSpec((None, 6, d), lambda bi, i: (bi, 0, 0)),
            _const_spec(g1.shape), _const_spec(win.shape), _const_spec(qg.shape), _const_spec(kg.shape),
            tab, tab,
            _const_spec(mqg.shape), _const_spec(wuq.shape), _const_spec(mkvg.shape), _const_spec(wukv.shape),
            tab, tab,
        ],
        out_specs=[row(w) for w in widths],
        out_shape=[jax.ShapeDtypeStruct((b, s, w), dt) for w, dt in zip(widths, dtypes)],
        compiler_params=_cparams(("arbitrary", "arbitrary")),
        name="prep",
    )(x, mod, g1, win, qg, kg, cosa, sina, mqg, wuq, mkvg, wukv, cosb, sinb)


def _attn_kernel(q_ref, k_ref, v_ref, o_ref, *, rep, dk, dv):
    k = k_ref[...]
    v = v_ref[...]
    for r in range(rep):
        s = _dot_nt(q_ref[:, r * dk:(r + 1) * dk], k)
        p = jnp.exp(s - jnp.max(s, axis=-1, keepdims=True))
        l = jnp.sum(p, axis=-1, keepdims=True)
        o = _dot(p.astype(BF16), v)
        o_ref[:, r * dv:(r + 1) * dv] = (o / l).astype(o_ref.dtype)


def _attention(q, k, v, *, groups, rep, dk, dv, tq):
    b, s, _ = q.shape
    return pl.pallas_call(
        functools.partial(_attn_kernel, rep=rep, dk=dk, dv=dv),
        grid=(b, groups, s // tq),
        in_specs=[
            pl.BlockSpec((None, tq, rep * dk), lambda bi, g, i: (bi, i, g)),
            pl.BlockSpec((None, s, dk), lambda bi, g, i: (bi, 0, g)),
            pl.BlockSpec((None, s, dv), lambda bi, g, i: (bi, 0, g)),
        ],
        out_specs=pl.BlockSpec((None, tq, rep * dv), lambda bi, g, i: (bi, i, g)),
        out_shape=jax.ShapeDtypeStruct((b, s, groups * rep * dv), BF16),
        compiler_params=_cparams(("arbitrary", "arbitrary", "arbitrary")),
        name="attention",
    )(q, k, v)


def _split3(a):
    hi = a.astype(BF16)
    r = a - hi.astype(F32)
    mid = r.astype(BF16)
    lo = (r - mid.astype(F32)).astype(BF16)
    return hi, mid, lo


def _head_cols(a, lo_lane):
    lane = lax.broadcasted_iota(jnp.int32, (a.shape[0], LANES), 1)
    first = lane < SSD_HEAD_DIM
    parts = []
    for p in range(SSD_GHEADS // 2):
        c0 = lo_lane + 2 * p
        parts.append(jnp.where(first, a[:, c0:c0 + 1], a[:, c0 + 1:c0 + 2]))
    return jnp.concatenate(parts, axis=1)


def _ssd_kernel(xbc_ref, z_ref, dt_ref, cw_ref, cb_ref, dtb_ref, alog_ref, dskip_ref, ng_ref, o_ref,
                xpad_ref, act_ref, y_ref, hf_ref, hb_ref):
    s = xbc_ref.shape[0]
    nc = s // SSD_CHUNK
    L = SSD_CHUNK
    G = SSD_GINNER

    zeros_pad = jnp.zeros((CONV_PAD_ROWS, SSD_GCONV), F32)
    xpad_ref[0:CONV_PAD_ROWS, :] = zeros_pad
    xpad_ref[CONV_PAD_ROWS + s:2 * CONV_PAD_ROWS + s, :] = zeros_pad
    xpad_ref[CONV_PAD_ROWS:CONV_PAD_ROWS + s, :] = xbc_ref[...].astype(F32)
    cw = cw_ref[...]
    cb = cb_ref[...]
    half = SSD_CONV // 2

    def conv_body(c, carry):
        base = pl.multiple_of(c * L, L)
        win = xpad_ref[pl.ds(base, L + 2 * CONV_PAD_ROWS), :]
        acc = cb
        for j in range(SSD_CONV):
            off = CONV_PAD_ROWS - half + j
            acc = acc + win[off:off + L, :] * cw[j:j + 1, :]
        act_ref[pl.ds(base, L), :] = _silu(acc)
        return carry

    lax.fori_loop(0, nc, conv_body, 0)

    row = lax.broadcasted_iota(jnp.int32, (L, L), 0)
    col = lax.broadcasted_iota(jnp.int32, (L, L), 1)
    lower = row >= col
    upper = row <= col
    tri = lower.astype(BF16)
    lane = lax.broadcasted_iota(jnp.int32, (L, LANES), 1)
    is_fwd_lane = lane < SSD_GHEADS
    first_half = lax.broadcasted_iota(jnp.int32, (L, LANES), 1) < SSD_HEAD_DIM
    aneg = -jnp.exp(alog_ref[...])
    dtb = dtb_ref[...]
    dskip = dskip_ref[...]
    hf_ref[...] = jnp.zeros_like(hf_ref)
    hb_ref[...] = jnp.zeros_like(hb_ref)
    neg_big = -1e30

    def chunk_terms(base):
        dtr = dt_ref[pl.ds(base, L), :] + dtb
        dtv = jnp.maximum(dtr, 0.0) + jnp.log1p(jnp.exp(-jnp.abs(dtr)))
        a = dtv * aneg
        hi, mid, lo = _split3(a)
        cs = _dot(tri, hi) + _dot(tri, mid) + _dot(tri, lo)
        tot = cs[L - 1:L, :]
        ex = cs - a
        d_in = jnp.exp(jnp.where(is_fwd_lane, cs, tot - ex))
        d_out = jnp.exp(jnp.where(is_fwd_lane, tot - cs, ex))
        pos = jnp.where(is_fwd_lane, cs, ex)
        return dtv, d_in, d_out, pos, jnp.exp(tot)

    def direction(base, h_ref, lo_lane, mask, sign):
        act = act_ref[pl.ds(base, L), :]
        xs = act[:, :G]
        bm = act[:, G:G + SSD_STATE]
        cm = act[:, G + SSD_STATE:]
        dtv, d_in, d_out, pos, cdec = chunk_terms(base)
        cmb = cm.astype(BF16)
        cbm = _dot_nt(cmb, bm.astype(BF16))
        post = pos.T
        dt_x = _head_cols(dtv, lo_lane)
        xd = xs * dt_x
        ys = []
        for p in range(SSD_GHEADS // 2):
            ws = []
            for hh in range(2):
                ln = lo_lane + 2 * p + hh
                seg = (pos[:, ln:ln + 1] - post[ln:ln + 1, :]) * sign
                ws.append((cbm * jnp.exp(jnp.where(mask, seg, neg_big))).astype(BF16))
            xp = xd[:, p * LANES:(p + 1) * LANES]
            rhs = jnp.concatenate([jnp.where(first_half, xp, 0.0), jnp.where(first_half, 0.0, xp)],
                                  axis=0).astype(BF16)
            ys.append(_dot(jnp.concatenate(ws, axis=1), rhs))
        y = jnp.concatenate(ys, axis=1)
        hprev = h_ref[...]
        y = y + _dot(cmb, hprev.astype(BF16)) * _head_cols(d_in, lo_lane)
        xo = (xd * _head_cols(d_out, lo_lane)).astype(BF16)
        st = _dot(bm.T.astype(BF16), xo)
        h_ref[...] = hprev * _head_cols(cdec, lo_lane) + st
        return y, xs

    def scan_body(c, carry):
        bf = pl.multiple_of(c * L, L)
        yf, xs = direction(bf, hf_ref, 0, lower, 1.0)
        y_ref[pl.ds(bf, L), :] = y_ref[pl.ds(bf, L), :] + yf + xs * dskip
        bb = pl.multiple_of((nc - 1 - c) * L, L)
        yb, _ = direction(bb, hb_ref, SSD_GHEADS, upper, -1.0)
        y_ref[pl.ds(bb, L), :] = y_ref[pl.ds(bb, L), :] + yb
        return carry

    y_ref[...] = jnp.zeros_like(y_ref)
    lax.fori_loop(0, nc, scan_body, 0)

    ng = ng_ref[...]

    def out_body(c, carry):
        base = pl.multiple_of(c * L, L)
        y = y_ref[pl.ds(base, L), :] * _silu(z_ref[pl.ds(base, L), :].astype(F32))
        o_ref[pl.ds(base, L), :] = _rms(y, ng).astype(o_ref.dtype)
        return carry

    lax.fori_loop(0, nc, out_body, 0)


def _ssd(xbc, z, dt, cw, cb, dtb, alog, dskip, ng):
    b, s, _ = xbc.shape
    par = lambda a: pl.BlockSpec((None,) + a.shape[1:], lambda bi, g: (g, 0, 0))
    return pl.pallas_call(
        _ssd_kernel,
        grid=(b, SSD_GROUPS),
        in_specs=[
            pl.BlockSpec((None, s, SSD_GCONV), lambda bi, g: (bi, 0, g)),
            pl.BlockSpec((None, s, SSD_GINNER), lambda bi, g: (bi, 0, g)),
            pl.BlockSpec((None, s, LANES), lambda bi, g: (bi, 0, g)),
            par(cw), par(cb), par(dtb), par(alog), par(dskip), par(ng),
        ],
        out_specs=pl.BlockSpec((None, s, SSD_GINNER), lambda bi, g: (bi, 0, g)),
        out_shape=jax.ShapeDtypeStruct((b, s, SSD_INNER), BF16),
        scratch_shapes=[
            pltpu.VMEM((s + 2 * CONV_PAD_ROWS, SSD_GCONV), F32),
            pltpu.VMEM((s, SSD_GCONV), F32),
            pltpu.VMEM((s, SSD_GINNER), F32),
            pltpu.VMEM((SSD_STATE, SSD_GINNER), F32),
            pltpu.VMEM((SSD_STATE, SSD_GINNER), F32),
        ],
        compiler_params=_cparams(("arbitrary", "arbitrary")),
        name="ssd",
    )(xbc, z, dt, cw, cb, dtb, alog, dskip, ng)


def _outproj_kernel(x_ref, mod_ref, oa_ref, ob_ref, oc_ref, w_ref, o_ref):
    gate = mod_ref[2:3, :]
    a1 = GQA_WIDTH
    a2 = GQA_WIDTH + MLA_WIDTH
    mix = _dot(oa_ref[...], w_ref[0:a1, :])
    mix = mix + _dot(ob_ref[...], w_ref[a1:a2, :])
    mix = mix + _dot(oc_ref[...], w_ref[a2:, :])
    o_ref[...] = x_ref[...] + gate * mix


def _outproj(x, mod, oa, ob, oc, w, tm):
    b, s, d = x.shape
    row = lambda wd: pl.BlockSpec((None, tm, wd), lambda bi, i: (bi, i, 0))
    return pl.pallas_call(
        _outproj_kernel,
        grid=(b, s // tm),
        in_specs=[row(d), pl.BlockSpec((None, 6, d), lambda bi, i: (bi, 0, 0)),
                  row(oa.shape[-1]), row(ob.shape[-1]), row(oc.shape[-1]), _const_spec(w.shape)],
        out_specs=row(d),
        out_shape=jax.ShapeDtypeStruct(x.shape, F32),
        compiler_params=_cparams(("arbitrary", "arbitrary")),
        name="outproj",
    )(x, mod, oa, ob, oc, w)


def _ffn_kernel(x_ref, mod_ref, g2_ref, wg_ref, wu_ref, wd_ref, fg_ref, o_ref, h_ref, acc_ref, *, final):
    j = pl.program_id(2)

    @pl.when(j == 0)
    def _():
        shift = mod_ref[3:4, :]
        scale = mod_ref[4:5, :]
        h_ref[...] = (_rms(x_ref[...], g2_ref[...]) * (1.0 + scale) + shift).astype(BF16)
        acc_ref[...] = jnp.zeros_like(acc_ref)

    h = h_ref[...]
    gate = _dot(h, wg_ref[...])
    up = _dot(h, wu_ref[...])
    acc_ref[...] += _dot((_silu(gate) * up).astype(BF16), wd_ref[...])

    @pl.when(j == pl.num_programs(2) - 1)
    def _():
        y = x_ref[...] + mod_ref[5:6, :] * acc_ref[...]
        if final:
            y = _rms(y, fg_ref[...])
        o_ref[...] = y


def _ffn(x, mod, g2, wgu, wd, fg, tm, th, final):
    b, s, d = x.shape
    hidden = wd.shape[0]
    nh = hidden // th
    row = pl.BlockSpec((None, tm, d), lambda bi, i, j: (bi, i, 0))
    return pl.pallas_call(
        functools.partial(_ffn_kernel, final=final),
        grid=(b, s // tm, nh),
        in_specs=[
            row,
            pl.BlockSpec((None, 6, d), lambda bi, i, j: (bi, 0, 0)),
            pl.BlockSpec((1, d), lambda bi, i, j: (0, 0)),
            pl.BlockSpec((d, th), lambda bi, i, j: (0, j)),
            pl.BlockSpec((d, th), lambda bi, i, j: (0, nh + j)),
            pl.BlockSpec((th, d), lambda bi, i, j: (j, 0)),
            pl.BlockSpec((1, d), lambda bi, i, j: (0, 0)),
        ],
        out_specs=row,
        out_shape=jax.ShapeDtypeStruct(x.shape, F32),
        scratch_shapes=[pltpu.VMEM((tm, d), BF16), pltpu.VMEM((tm, d), F32)],
        compiler_params=_cparams(("arbitrary", "arbitrary", "arbitrary")),
        name="ffn",
    )(x, mod, g2, wgu, wgu, wd, fg)


def _rope_tables(seq_len, rot_dim):
    rows = seq_len // GRID_W
    row_idx = jnp.repeat(jnp.arange(rows), GRID_W).astype(F32)
    col_idx = jnp.tile(jnp.arange(GRID_W), rows).astype(F32)
    axis_dim = rot_dim // 2
    inv_freq = jnp.power(ROPE_THETA, -jnp.arange(0, axis_dim, 2, dtype=F32) / axis_dim)
    ang_r = row_idx[:, None] * inv_freq[None, :]
    ang_c = col_idx[:, None] * inv_freq[None, :]
    cos = jnp.concatenate([jnp.cos(ang_r)] * 2 + [jnp.cos(ang_c)] * 2, axis=-1)
    sin = jnp.concatenate([-jnp.sin(ang_r), jnp.sin(ang_r), -jnp.sin(ang_c), jnp.sin(ang_c)], axis=-1)
    pad = LANES - rot_dim
    if pad:
        cos = jnp.pad(cos, ((0, 0), (0, pad)))
        sin = jnp.pad(sin, ((0, 0), (0, pad)))
    return cos, sin


def _in_proj_perm():
    o_q = 0
    o_k = o_q + GQA_WIDTH
    o_v = o_k + GQA_KV_WIDTH
    o_cq = o_v + GQA_KV_WIDTH
    o_ckv = o_cq + MLA_Q_LORA
    o_kpe = o_ckv + MLA_KV_LORA
    o_z = o_kpe + MLA_ROPE_DIM
    o_xs = o_z + SSD_INNER
    o_bm = o_xs + SSD_INNER
    o_cm = o_bm + SSD_GROUPS * SSD_STATE
    o_dt = o_cm + SSD_GROUPS * SSD_STATE
    ar = lambda a, n: list(range(a, a + n))
    perm = []
    for g in range(SSD_GROUPS):
        perm += ar(o_xs + g * SSD_GINNER, SSD_GINNER) + ar(o_bm + g * SSD_STATE, SSD_STATE)
        perm += ar(o_cm + g * SSD_STATE, SSD_STATE)
    perm += ar(o_z, SSD_INNER)
    for g in range(SSD_GROUPS):
        perm += ar(o_dt + g * SSD_GHEADS, SSD_GHEADS) + ar(o_dt + SSD_HEADS + g * SSD_GHEADS, SSD_GHEADS)
        perm += [-1] * (LANES - 2 * SSD_GHEADS)
    perm += ar(o_q, GQA_WIDTH) + ar(o_k, GQA_KV_WIDTH) + ar(o_v, GQA_KV_WIDTH)
    perm += ar(o_cq, MLA_Q_LORA) + ar(o_ckv, MLA_KV_LORA)
    perm += ar(o_kpe, MLA_ROPE_DIM) + [-1] * (LANES - MLA_ROPE_DIM)
    assert len(perm) == IN_COLS_PAD
    return np.asarray(perm, np.int32), o_dt + 2 * SSD_HEADS


def _take_cols(w, perm):
    n = w.shape[-1]
    wz = jnp.concatenate([w, jnp.zeros(w.shape[:-1] + (1,), w.dtype)], axis=-1)
    return jnp.take(wz, jnp.asarray(np.where(perm < 0, n, perm)), axis=-1)


def _group_heads(a):
    depth = a.shape[0]
    a = a.reshape(depth, 2, SSD_GROUPS, SSD_GHEADS).transpose(0, 2, 1, 3).reshape(depth, SSD_GROUPS, 2 * SSD_GHEADS)
    return jnp.pad(a, ((0, 0), (0, 0), (0, LANES - 2 * SSD_GHEADS)))[:, :, None, :]


def kernel(x, c, w_ada, b_ada, norm1_g, norm2_g, w_in, q_norm_g, k_norm_g, mla_q_norm_g, w_uq, mla_kv_norm_g,
           w_ukv, conv_w, conv_b, dt_bias, a_log, d_skip, ssd_norm_g, w_out, w_gate_up, w_down, final_norm_g):
    b, s, d = x.shape
    depth = w_in.shape[0]
    tm = min(512, s)
    tq = min(256, s)
    th = 512

    perm, n_in = _in_proj_perm()
    assert n_in == w_in.shape[-1]
    win_p = _take_cols(w_in, perm).astype(BF16)
    uq_perm = np.concatenate([np.concatenate([np.arange(h * MLA_QK_DIM, (h + 1) * MLA_QK_DIM),
                                              -np.ones(MLA_QK_PAD - MLA_QK_DIM, np.int64)])
                              for h in range(MLA_HEADS)]).astype(np.int32)
    wuq_p = _take_cols(w_uq, uq_perm).astype(BF16)
    hw = MLA_NOPE_DIM + MLA_V_DIM
    ukv_perm = np.concatenate([np.arange(h * hw, h * hw + MLA_NOPE_DIM) for h in range(MLA_HEADS)]
                              + [np.arange(h * hw + MLA_NOPE_DIM, (h + 1) * hw) for h in range(MLA_HEADS)])
    wukv_p = jnp.take(w_ukv, jnp.asarray(ukv_perm.astype(np.int32)), axis=-1).astype(BF16)
    conv_perm = np.concatenate([np.concatenate([np.arange(g * SSD_GINNER, (g + 1) * SSD_GINNER),
                                                SSD_INNER + np.arange(g * SSD_STATE, (g + 1) * SSD_STATE),
                                                SSD_INNER + SSD_GROUPS * SSD_STATE
                                                + np.arange(g * SSD_STATE, (g + 1) * SSD_STATE)])
                                for g in range(SSD_GROUPS)]).astype(np.int32)
    cw_p = jnp.take(conv_w, jnp.asarray(conv_perm), axis=-1).reshape(depth, SSD_CONV, SSD_GROUPS, SSD_GCONV)
    cw_p = cw_p.transpose(0, 2, 1, 3)
    cb_p = jnp.take(conv_b, jnp.asarray(conv_perm), axis=-1).reshape(depth, SSD_GROUPS, 1, SSD_GCONV)
    dtb_p = _group_heads(dt_bias)
    alog_p = _group_heads(a_log)
    dskip_p = jnp.repeat(d_skip, SSD_HEAD_DIM, axis=-1).reshape(depth, SSD_GROUPS, 1, SSD_GINNER)
    ng_p = ssd_norm_g.reshape(depth, SSD_GROUPS, 1, SSD_GINNER)
    wout_b = w_out.astype(BF16)
    wgu_b = w_gate_up.astype(BF16)
    wd_b = w_down.astype(BF16)
    cosa, sina = _rope_tables(s, GQA_HEAD_DIM)
    cosb, sinb = _rope_tables(s, MLA_ROPE_DIM)

    mod = _modulation(c, w_ada, b_ada).reshape(depth, b, 6, d)
    fg = final_norm_g.reshape(1, d)

    for l in range(depth):
        xbc, z, dt, qa, ka, va, qb, kb, vb = _prep(
            x, mod[l], norm1_g[l].reshape(1, d), win_p[l], q_norm_g[l].reshape(1, -1), k_norm_g[l].reshape(1, -1),
            cosa, sina, mla_q_norm_g[l].reshape(1, -1), wuq_p[l], mla_kv_norm_g[l].reshape(1, -1), wukv_p[l],
            cosb, sinb, tm)
        oa = _attention(qa, ka, va, groups=GQA_KV_HEADS, rep=GQA_HEADS // GQA_KV_HEADS,
                        dk=GQA_HEAD_DIM, dv=GQA_HEAD_DIM, tq=tq)
        ob = _attention(qb, kb, vb, groups=MLA_HEADS, rep=1, dk=MLA_QK_PAD, dv=MLA_V_DIM, tq=tq)
        oc = _ssd(xbc, z, dt, cw_p[l], cb_p[l], dtb_p[l], alog_p[l], dskip_p[l], ng_p[l])
        x = _outproj(x, mod[l], oa, ob, oc, wout_b[l], tm)
        x = _ffn(x, mod[l], norm2_g[l].reshape(1, d), wgu_b[l], wd_b[l], fg, tm, th, final=(l == depth - 1))
    return x
```

```python
import functools

import numpy as np
import jax
import jax.numpy as jnp
from jax import lax
from jax.experimental import pallas as pl
from jax.experimental.pallas import tpu as pltpu

F32 = jnp.float32
BF16 = jnp.bfloat16

GRID_W = 64
ROPE_THETA = 10000.0
EPS = 1e-6
LOG2E = 1.4426950408889634

GQA_HEADS = 6
GQA_KV_HEADS = 2
GQA_HEAD_DIM = 128
GQA_WIDTH = GQA_HEADS * GQA_HEAD_DIM
GQA_KV_WIDTH = GQA_KV_HEADS * GQA_HEAD_DIM

MLA_HEADS = 4
MLA_Q_LORA = 512
MLA_KV_LORA = 256
MLA_NOPE_DIM = 128
MLA_ROPE_DIM = 64
MLA_V_DIM = 128
MLA_QK_DIM = MLA_NOPE_DIM + MLA_ROPE_DIM
MLA_QK_PAD = 256
MLA_WIDTH = MLA_HEADS * MLA_V_DIM

SSD_HEADS = 12
SSD_HEAD_DIM = 64
SSD_GROUPS = 2
SSD_STATE = 128
SSD_CONV = 5
SSD_CHUNK = 128
SSD_INNER = SSD_HEADS * SSD_HEAD_DIM
SSD_CONV_DIM = SSD_INNER + 2 * SSD_GROUPS * SSD_STATE
SSD_GHEADS = SSD_HEADS // SSD_GROUPS
SSD_GINNER = SSD_INNER // SSD_GROUPS
SSD_GCONV = SSD_GINNER + 2 * SSD_STATE

LANES = 128
CONV_PAD_ROWS = 8
ATTN_KEY_CHUNK = 512

SEG_XBC = (0, SSD_GROUPS * SSD_GCONV)
SEG_Z = (SEG_XBC[1], SEG_XBC[1] + SSD_INNER)
SEG_DT = (SEG_Z[1], SEG_Z[1] + SSD_GROUPS * LANES)
SEG_QA = (SEG_DT[1], SEG_DT[1] + GQA_WIDTH)
SEG_KA = (SEG_QA[1], SEG_QA[1] + GQA_KV_WIDTH)
SEG_VA = (SEG_KA[1], SEG_KA[1] + GQA_KV_WIDTH)
SEG_CQ = (SEG_VA[1], SEG_VA[1] + MLA_Q_LORA)
SEG_CKV = (SEG_CQ[1], SEG_CQ[1] + MLA_KV_LORA)
SEG_KPE = (SEG_CKV[1], SEG_CKV[1] + LANES)
IN_COLS_PAD = SEG_KPE[1]

VMEM_LIMIT = 56 * 1024 * 1024


def _cparams(sem):
    return pltpu.CompilerParams(dimension_semantics=sem, vmem_limit_bytes=VMEM_LIMIT)


def _dot(a, b):
    return jnp.dot(a, b, preferred_element_type=F32)


def _dot_nt(a, b):
    return lax.dot_general(a, b, (((1,), (1,)), ((), ())), preferred_element_type=F32)


def _rms(x, g):
    return x * lax.rsqrt(jnp.mean(x * x, axis=-1, keepdims=True) + EPS) * g


def _silu(x):
    return x * (1.0 / (1.0 + jnp.exp(-x)))


def _swap_quarters(x, q):
    lane = lax.broadcasted_iota(jnp.int32, x.shape, 1)
    even = (lane // q) % 2 == 0
    return jnp.where(even, pltpu.roll(x, LANES - q, 1), pltpu.roll(x, q, 1))


def _rope(x, cos, sin, q):
    return x * cos + _swap_quarters(x, q) * sin


def _mod_kernel(c_ref, w_ref, b_ref, o_ref):
    c = c_ref[...]
    act = _silu(c).astype(BF16)
    o_ref[...] = _dot(act, w_ref[...].astype(BF16)) + b_ref[...]


def _modulation(c, w_ada, b_ada):
    depth, d, n = w_ada.shape
    b = c.shape[0]
    tn = 1024
    return pl.pallas_call(
        _mod_kernel,
        grid=(depth, n // tn),
        in_specs=[
            pl.BlockSpec((b, d), lambda l, j: (0, 0)),
            pl.BlockSpec((None, d, tn), lambda l, j: (l, 0, j)),
            pl.BlockSpec((None, 1, tn), lambda l, j: (l, 0, j)),
        ],
        out_specs=pl.BlockSpec((None, b, tn), lambda l, j: (l, 0, j)),
        out_shape=jax.ShapeDtypeStruct((depth, b, n), F32),
        compiler_params=_cparams(("arbitrary", "arbitrary")),
        name="modulation",
    )(c, w_ada, b_ada.reshape(depth, 1, n))


def _prep_kernel(x_ref, mod_ref, g1_ref, win_ref, qg_ref, kg_ref, cosa_ref, sina_ref,
                 mqg_ref, wuq_ref, mkvg_ref, wukv_ref, cosb_ref, sinb_ref,
                 xbc_ref, z_ref, dt_ref, qat_ref, ka_ref, vat_ref, qbt_ref, kb_ref, vbt_ref):
    x = x_ref[...]
    shift = mod_ref[0:1, :]
    scale = mod_ref[1:2, :]
    h = (_rms(x, g1_ref[...]) * (1.0 + scale) + shift).astype(BF16)

    def proj(seg):
        return _dot(h, win_ref[:, seg[0]:seg[1]])

    xbc_ref[...] = proj(SEG_XBC).astype(BF16)
    z_ref[...] = proj(SEG_Z).astype(BF16)
    dt_ref[...] = proj(SEG_DT)

    def put_t(ref, blk, val):
        ref[blk * LANES:(blk + 1) * LANES, :] = val.T.astype(BF16)

    va = proj(SEG_VA)
    for hd in range(GQA_KV_HEADS):
        put_t(vat_ref, hd, va[:, hd * GQA_HEAD_DIM:(hd + 1) * GQA_HEAD_DIM])

    cosa = cosa_ref[...]
    sina = sina_ref[...]
    qa = proj(SEG_QA)
    qscale = GQA_HEAD_DIM ** -0.5 * LOG2E
    for hd in range(GQA_HEADS):
        sl = slice(hd * GQA_HEAD_DIM, (hd + 1) * GQA_HEAD_DIM)
        q = _rope(_rms(qa[:, sl], qg_ref[...]), cosa, sina, GQA_HEAD_DIM // 4)
        put_t(qat_ref, hd, q * qscale)
    ka = proj(SEG_KA)
    for hd in range(GQA_KV_HEADS):
        sl = slice(hd * GQA_HEAD_DIM, (hd + 1) * GQA_HEAD_DIM)
        k = _rope(_rms(ka[:, sl], kg_ref[...]), cosa, sina, GQA_HEAD_DIM // 4)
        ka_ref[:, sl] = k.astype(BF16)

    cosb = cosb_ref[...]
    sinb = sinb_ref[...]
    cq = _rms(proj(SEG_CQ), mqg_ref[...]).astype(BF16)
    qb = _dot(cq, wuq_ref[...])
    bscale = MLA_QK_DIM ** -0.5 * LOG2E
    for hd in range(MLA_HEADS):
        c0 = hd * MLA_QK_PAD
        put_t(qbt_ref, 2 * hd, qb[:, c0:c0 + LANES] * bscale)
        qpe = _rope(qb[:, c0 + LANES:c0 + 2 * LANES], cosb, sinb, MLA_ROPE_DIM // 4)
        put_t(qbt_ref, 2 * hd + 1, qpe * bscale)
    ckv = _rms(proj(SEG_CKV), mkvg_ref[...]).astype(BF16)
    kv = _dot(ckv, wukv_ref[...])
    kpe = _rope(proj(SEG_KPE), cosb, sinb, MLA_ROPE_DIM // 4).astype(BF16)
    for hd in range(MLA_HEADS):
        c0 = hd * MLA_QK_PAD
        kb_ref[:, c0:c0 + LANES] = kv[:, hd * LANES:(hd + 1) * LANES].astype(BF16)
        kb_ref[:, c0 + LANES:c0 + 2 * LANES] = kpe
        put_t(vbt_ref, hd, kv[:, (MLA_HEADS + hd) * LANES:(MLA_HEADS + hd + 1) * LANES])


def _const_spec(shape):
    nd = len(shape)
    return pl.BlockSpec(shape, lambda *_: (0,) * nd, pipeline_mode=pl.Buffered(1))


def _prep(x, mod, g1, win, qg, kg, cosa, sina, mqg, wuq, mkvg, wukv, cosb, sinb, tm):
    b, s, d = x.shape
    row = lambda w: pl.BlockSpec((None, tm, w), lambda bi, i: (bi, i, 0))
    tab = pl.BlockSpec((tm, LANES), lambda bi, i: (i, 0))
    col = lambda w: pl.BlockSpec((None, w, tm), lambda bi, i: (bi, 0, i))
    outs = ((SSD_CONV_DIM, BF16, False), (SSD_INNER, BF16, False), (SSD_GROUPS * LANES, F32, False),
            (GQA_WIDTH, BF16, True), (GQA_KV_WIDTH, BF16, False), (GQA_KV_WIDTH, BF16, True),
            (MLA_HEADS * MLA_QK_PAD, BF16, True), (MLA_HEADS * MLA_QK_PAD, BF16, False), (MLA_WIDTH, BF16, True))
    return pl.pallas_call(
        _prep_kernel,
        grid=(b, s // tm),
        in_specs=[
            row(d),
            pl.BlockSpec((None, 6, d), lambda bi, i: (bi, 0, 0)),
            _const_spec(g1.shape), _const_spec(win.shape), _const_spec(qg.shape), _const_spec(kg.shape),
            tab, tab,
            _const_spec(mqg.shape), _const_spec(wuq.shape), _const_spec(mkvg.shape), _const_spec(wukv.shape),
            tab, tab,
        ],
        out_specs=[col(w) if t else row(w) for w, _, t in outs],
        out_shape=[jax.ShapeDtypeStruct((b, w, s) if t else (b, s, w), dt) for w, dt, t in outs],
        compiler_params=_cparams(("arbitrary", "arbitrary")),
        name="prep",
    )(x, mod, g1, win, qg, kg, cosa, sina, mqg, wuq, mkvg, wukv, cosb, sinb)


def _attn_kernel(qt_ref, k_ref, vt_ref, o_ref, *, heads, shared_kv, dk, dv):
    s_len = k_ref.shape[0]
    kc = min(ATTN_KEY_CHUNK, s_len)
    nchunk = s_len // kc
    stages = [(r, c) for r in range(heads) for c in range(nchunk)]

    def scores(r, c):
        kv = 0 if shared_kv else r
        return _dot(k_ref[c * kc:(c + 1) * kc, kv * dk:(kv + 1) * dk], qt_ref[r * dk:(r + 1) * dk, :])

    st_next = scores(*stages[0])
    for i, (r, c) in enumerate(stages):
        st = st_next
        if i + 1 < len(stages):
            st_next = scores(*stages[i + 1])
        kv = 0 if shared_kv else r
        vt = vt_ref[kv * dv:(kv + 1) * dv, c * kc:(c + 1) * kc]
        mc = jnp.max(st, axis=0, keepdims=True)
        if c == 0:
            m = mc
            p = jnp.exp2(st - m)
            l = jnp.sum(p, axis=0, keepdims=True)
            acc = _dot(vt, p.astype(BF16))
        else:
            m_new = jnp.maximum(m, mc)
            alpha = jnp.exp2(m - m_new)
            p = jnp.exp2(st - m_new)
            l = alpha * l + jnp.sum(p, axis=0, keepdims=True)
            acc = alpha * acc + _dot(vt, p.astype(BF16))
            m = m_new
        if c == nchunk - 1:
            o_ref[:, r * dv:(r + 1) * dv] = (acc * (1.0 / l)).T.astype(o_ref.dtype)


def _attention(qt, k, vt, *, groups, heads, shared_kv, dk, dv, tq):
    b, _, s = qt.shape
    nkv = 1 if shared_kv else heads
    return pl.pallas_call(
        functools.partial(_attn_kernel, heads=heads, shared_kv=shared_kv, dk=dk, dv=dv),
        grid=(b, groups, s // tq),
        in_specs=[
            pl.BlockSpec((None, heads * dk, tq), lambda bi, g, i: (bi, g, i)),
            pl.BlockSpec((None, s, nkv * dk), lambda bi, g, i: (bi, 0, g)),
            pl.BlockSpec((None, nkv * dv, s), lambda bi, g, i: (bi, g, 0)),
        ],
        out_specs=pl.BlockSpec((None, tq, heads * dv), lambda bi, g, i: (bi, i, g)),
        out_shape=jax.ShapeDtypeStruct((b, s, groups * heads * dv), BF16),
        compiler_params=_cparams(("arbitrary", "arbitrary", "arbitrary")),
        name="attention",
    )(qt, k, vt)


def _split3(a):
    hi = a.astype(BF16)
    r = a - hi.astype(F32)
    mid = r.astype(BF16)
    lo = (r - mid.astype(F32)).astype(BF16)
    return hi, mid, lo


def _head_cols(a, lo_lane):
    lane = lax.broadcasted_iota(jnp.int32, (a.shape[0], LANES), 1)
    first = lane < SSD_HEAD_DIM
    parts = []
    for p in range(SSD_GHEADS // 2):
        c0 = lo_lane + 2 * p
        parts.append(jnp.where(first, a[:, c0:c0 + 1], a[:, c0 + 1:c0 + 2]))
    return jnp.concatenate(parts, axis=1)


def _ssd_kernel(xbc_ref, z_ref, dt_ref, cw_ref, cb_ref, dtb_ref, alog_ref, dskip_ref, ng_ref, o_ref,
                xpad_ref, act_ref, y_ref, hf_ref, hb_ref):
    s = xbc_ref.shape[0]
    nc = s // SSD_CHUNK
    L = SSD_CHUNK
    G = SSD_GINNER

    zeros_pad = jnp.zeros((CONV_PAD_ROWS, SSD_GCONV), F32)
    xpad_ref[0:CONV_PAD_ROWS, :] = zeros_pad
    xpad_ref[CONV_PAD_ROWS + s:2 * CONV_PAD_ROWS + s, :] = zeros_pad
    xpad_ref[CONV_PAD_ROWS:CONV_PAD_ROWS + s, :] = xbc_ref[...].astype(F32)
    cw = cw_ref[...]
    cb = cb_ref[...]
    half = SSD_CONV // 2

    def conv_body(c, carry):
        base = pl.multiple_of(c * L, L)
        win = xpad_ref[pl.ds(base, L + 2 * CONV_PAD_ROWS), :]
        acc = cb
        for j in range(SSD_CONV):
            off = CONV_PAD_ROWS - half + j
            acc = acc + win[off:off + L, :] * cw[j:j + 1, :]
        act_ref[pl.ds(base, L), :] = _silu(acc)
        return carry

    lax.fori_loop(0, nc, conv_body, 0)

    row = lax.broadcasted_iota(jnp.int32, (L, L), 0)
    col = lax.broadcasted_iota(jnp.int32, (L, L), 1)
    lower = row >= col
    upper = row <= col
    tri = lower.astype(BF16)
    lane = lax.broadcasted_iota(jnp.int32, (L, LANES), 1)
    is_fwd_lane = lane < SSD_GHEADS
    first_half = lax.broadcasted_iota(jnp.int32, (L, LANES), 1) < SSD_HEAD_DIM
    aneg = -jnp.exp(alog_ref[...])
    dtb = dtb_ref[...]
    dskip = dskip_ref[...]
    hf_ref[...] = jnp.zeros_like(hf_ref)
    hb_ref[...] = jnp.zeros_like(hb_ref)
    neg_big = -1e30

    def chunk_terms(base):
        dtr = dt_ref[pl.ds(base, L), :] + dtb
        dtv = jnp.maximum(dtr, 0.0) + jnp.log1p(jnp.exp(-jnp.abs(dtr)))
        a = dtv * aneg
        hi, mid, lo = _split3(a)
        cs = _dot(tri, hi) + _dot(tri, mid) + _dot(tri, lo)
        tot = cs[L - 1:L, :]
        ex = cs - a
        d_in = jnp.exp(jnp.where(is_fwd_lane, cs, tot - ex))
        d_out = jnp.exp(jnp.where(is_fwd_lane, tot - cs, ex))
        pos = jnp.where(is_fwd_lane, cs, ex)
        return dtv, d_in, d_out, pos, jnp.exp(tot)

    def direction(base, h_ref, lo_lane, mask, sign):
        act = act_ref[pl.ds(base, L), :]
        xs = act[:, :G]
        bm = act[:, G:G + SSD_STATE]
        cm = act[:, G + SSD_STATE:]
        dtv, d_in, d_out, pos, cdec = chunk_terms(base)
        cmb = cm.astype(BF16)
        cbm = _dot_nt(cmb, bm.astype(BF16))
        post = pos.T
        dt_x = _head_cols(dtv, lo_lane)
        xd = xs * dt_x
        ys = []
        for p in range(SSD_GHEADS // 2):
            ws = []
            for hh in range(2):
                ln = lo_lane + 2 * p + hh
                seg = (pos[:, ln:ln + 1] - post[ln:ln + 1, :]) * sign
                ws.append((cbm * jnp.exp(jnp.where(mask, seg, neg_big))).astype(BF16))
            xp = xd[:, p * LANES:(p + 1) * LANES]
            rhs = jnp.concatenate([jnp.where(first_half, xp, 0.0), jnp.where(first_half, 0.0, xp)],
                                  axis=0).astype(BF16)
            ys.append(_dot(jnp.concatenate(ws, axis=1), rhs))
        y = jnp.concatenate(ys, axis=1)
        hprev = h_ref[...]
        y = y + _dot(cmb, hprev.astype(BF16)) * _head_cols(d_in, lo_lane)
        xo = (xd * _head_cols(d_out, lo_lane)).astype(BF16)
        st = _dot(bm.T.astype(BF16), xo)
        h_ref[...] = hprev * _head_cols(cdec, lo_lane) + st
        return y, xs

    def scan_body(c, carry):
        bf = pl.multiple_of(c * L, L)
        yf, xs = direction(bf, hf_ref, 0, lower, 1.0)
        y_ref[pl.ds(bf, L), :] = y_ref[pl.ds(bf, L), :] + yf + xs * dskip
        bb = pl.multiple_of((nc - 1 - c) * L, L)
        yb, _ = direction(bb, hb_ref, SSD_GHEADS, upper, -1.0)
        y_ref[pl.ds(bb, L), :] = y_ref[pl.ds(bb, L), :] + yb
        return carry

    y_ref[...] = jnp.zeros_like(y_ref)
    lax.fori_loop(0, nc, scan_body, 0)

    ng = ng_ref[...]

    def out_body(c, carry):
        base = pl.multiple_of(c * L, L)
        y = y_ref[pl.ds(base, L), :] * _silu(z_ref[pl.ds(base, L), :].astype(F32))
        o_ref[pl.ds(base, L), :] = _rms(y, ng).astype(o_ref.dtype)
        return carry

    lax.fori_loop(0, nc, out_body, 0)


def _ssd(xbc, z, dt, cw, cb, dtb, alog, dskip, ng):
    b, s, _ = xbc.shape
    par = lambda a: pl.BlockSpec((None,) + a.shape[1:], lambda bi, g: (g, 0, 0))
    return pl.pallas_call(
        _ssd_kernel,
        grid=(b, SSD_GROUPS),
        in_specs=[
            pl.BlockSpec((None, s, SSD_GCONV), lambda bi, g: (bi, 0, g)),
            pl.BlockSpec((None, s, SSD_GINNER), lambda bi, g: (bi, 0, g)),
            pl.BlockSpec((None, s, LANES), lambda bi, g: (bi, 0, g)),
            par(cw), par(cb), par(dtb), par(alog), par(dskip), par(ng),
        ],
        out_specs=pl.BlockSpec((None, s, SSD_GINNER), lambda bi, g: (bi, 0, g)),
        out_shape=jax.ShapeDtypeStruct((b, s, SSD_INNER), BF16),
        scratch_shapes=[
            pltpu.VMEM((s + 2 * CONV_PAD_ROWS, SSD_GCONV), F32),
            pltpu.VMEM((s, SSD_GCONV), F32),
            pltpu.VMEM((s, SSD_GINNER), F32),
            pltpu.VMEM((SSD_STATE, SSD_GINNER), F32),
            pltpu.VMEM((SSD_STATE, SSD_GINNER), F32),
        ],
        compiler_params=_cparams(("arbitrary", "arbitrary")),
        name="ssd",
    )(xbc, z, dt, cw, cb, dtb, alog, dskip, ng)


def _outproj_kernel(x_ref, mod_ref, oa_ref, ob_ref, oc_ref, w_ref, o_ref):
    gate = mod_ref[2:3, :]
    a1 = GQA_WIDTH
    a2 = GQA_WIDTH + MLA_WIDTH
    mix = _dot(oa_ref[...], w_ref[0:a1, :])
    mix = mix + _dot(ob_ref[...], w_ref[a1:a2, :])
    mix = mix + _dot(oc_ref[...], w_ref[a2:, :])
    o_ref[...] = x_ref[...] + gate * mix


def _outproj(x, mod, oa, ob, oc, w, tm):
    b, s, d = x.shape
    row = lambda wd: pl.BlockSpec((None, tm, wd), lambda bi, i: (bi, i, 0))
    return pl.pallas_call(
        _outproj_kernel,
        grid=(b, s // tm),
        in_specs=[row(d), pl.BlockSpec((None, 6, d), lambda bi, i: (bi, 0, 0)),
                  row(oa.shape[-1]), row(ob.shape[-1]), row(oc.shape[-1]), _const_spec(w.shape)],
        out_specs=row(d),
        out_shape=jax.ShapeDtypeStruct(x.shape, F32),
        compiler_params=_cparams(("arbitrary", "arbitrary")),
        name="outproj",
    )(x, mod, oa, ob, oc, w)


def _ffn_kernel(x_ref, mod_ref, g2_ref, wg_ref, wu_ref, wd_ref, fg_ref, o_ref, h_ref, acc_ref, *, final):
    j = pl.program_id(2)

    @pl.when(j == 0)
    def _():
        shift = mod_ref[3:4, :]
        scale = mod_ref[4:5, :]
        h_ref[...] = (_rms(x_ref[...], g2_ref[...]) * (1.0 + scale) + shift).astype(BF16)
        acc_ref[...] = jnp.zeros_like(acc_ref)

    h = h_ref[...]
    gate = _dot(h, wg_ref[...])
    up = _dot(h, wu_ref[...])
    acc_ref[...] += _dot((_silu(gate) * up).astype(BF16), wd_ref[...])

    @pl.when(j == pl.num_programs(2) - 1)
    def _():
        y = x_ref[...] + mod_ref[5:6, :] * acc_ref[...]
        if final:
            y = _rms(y, fg_ref[...])
        o_ref[...] = y


def _ffn(x, mod, g2, wgu, wd, fg, tm, th, final):
    b, s, d = x.shape
    hidden = wd.shape[0]
    nh = hidden // th
    row = pl.BlockSpec((None, tm, d), lambda bi, i, j: (bi, i, 0))
    return pl.pallas_call(
        functools.partial(_ffn_kernel, final=final),
        grid=(b, s // tm, nh),
        in_specs=[
            row,
            pl.BlockSpec((None, 6, d), lambda bi, i, j: (bi, 0, 0)),
            pl.BlockSpec((1, d), lambda bi, i, j: (0, 0)),
            pl.BlockSpec((d, th), lambda bi, i, j: (0, j)),
            pl.BlockSpec((d, th), lambda bi, i, j: (0, nh + j)),
            pl.BlockSpec((th, d), lambda bi, i, j: (j, 0)),
            pl.BlockSpec((1, d), lambda bi, i, j: (0, 0)),
        ],
        out_specs=row,
        out_shape=jax.ShapeDtypeStruct(x.shape, F32),
        scratch_shapes=[pltpu.VMEM((tm, d), BF16), pltpu.VMEM((tm, d), F32)],
        compiler_params=_cparams(("arbitrary", "arbitrary", "arbitrary")),
        name="ffn",
    )(x, mod, g2, wgu, wgu, wd, fg)


def _rope_tables(seq_len, rot_dim):
    rows = seq_len // GRID_W
    row_idx = jnp.repeat(jnp.arange(rows), GRID_W).astype(F32)
    col_idx = jnp.tile(jnp.arange(GRID_W), rows).astype(F32)
    axis_dim = rot_dim // 2
    inv_freq = jnp.power(ROPE_THETA, -jnp.arange(0, axis_dim, 2, dtype=F32) / axis_dim)
    ang_r = row_idx[:, None] * inv_freq[None, :]
    ang_c = col_idx[:, None] * inv_freq[None, :]
    cos = jnp.concatenate([jnp.cos(ang_r)] * 2 + [jnp.cos(ang_c)] * 2, axis=-1)
    sin = jnp.concatenate([-jnp.sin(ang_r), jnp.sin(ang_r), -jnp.sin(ang_c), jnp.sin(ang_c)], axis=-1)
    pad = LANES - rot_dim
    if pad:
        cos = jnp.pad(cos, ((0, 0), (0, pad)))
        sin = jnp.pad(sin, ((0, 0), (0, pad)))
    return cos, sin


def _in_proj_perm():
    o_q = 0
    o_k = o_q + GQA_WIDTH
    o_v = o_k + GQA_KV_WIDTH
    o_cq = o_v + GQA_KV_WIDTH
    o_ckv = o_cq + MLA_Q_LORA
    o_kpe = o_ckv + MLA_KV_LORA
    o_z = o_kpe + MLA_ROPE_DIM
    o_xs = o_z + SSD_INNER
    o_bm = o_xs + SSD_INNER
    o_cm = o_bm + SSD_GROUPS * SSD_STATE
    o_dt = o_cm + SSD_GROUPS * SSD_STATE
    ar = lambda a, n: list(range(a, a + n))
    perm = []
    for g in range(SSD_GROUPS):
        perm += ar(o_xs + g * SSD_GINNER, SSD_GINNER) + ar(o_bm + g * SSD_STATE, SSD_STATE)
        perm += ar(o_cm + g * SSD_STATE, SSD_STATE)
    perm += ar(o_z, SSD_INNER)
    for g in range(SSD_GROUPS):
        perm += ar(o_dt + g * SSD_GHEADS, SSD_GHEADS) + ar(o_dt + SSD_HEADS + g * SSD_GHEADS, SSD_GHEADS)
        perm += [-1] * (LANES - 2 * SSD_GHEADS)
    perm += ar(o_q, GQA_WIDTH) + ar(o_k, GQA_KV_WIDTH) + ar(o_v, GQA_KV_WIDTH)
    perm += ar(o_cq, MLA_Q_LORA) + ar(o_ckv, MLA_KV_LORA)
    perm += ar(o_kpe, MLA_ROPE_DIM) + [-1] * (LANES - MLA_ROPE_DIM)
    assert len(perm) == IN_COLS_PAD
    return np.asarray(perm, np.int32), o_dt + 2 * SSD_HEADS


def _take_cols(w, perm):
    parts = []
    i = 0
    while i < len(perm):
        j = i + 1
        if perm[i] < 0:
            while j < len(perm) and perm[j] < 0:
                j += 1
            parts.append(jnp.zeros(w.shape[:-1] + (j - i,), w.dtype))
        else:
            while j < len(perm) and perm[j] == perm[j - 1] + 1:
                j += 1
            parts.append(w[..., int(perm[i]):int(perm[i]) + (j - i)])
        i = j
    return jnp.concatenate(parts, axis=-1)


def _group_heads(a):
    depth = a.shape[0]
    a = a.reshape(depth, 2, SSD_GROUPS, SSD_GHEADS).transpose(0, 2, 1, 3).reshape(depth, SSD_GROUPS, 2 * SSD_GHEADS)
    return jnp.pad(a, ((0, 0), (0, 0), (0, LANES - 2 * SSD_GHEADS)))[:, :, None, :]


def kernel(x, c, w_ada, b_ada, norm1_g, norm2_g, w_in, q_norm_g, k_norm_g, mla_q_norm_g, w_uq, mla_kv_norm_g,
           w_ukv, conv_w, conv_b, dt_bias, a_log, d_skip, ssd_norm_g, w_out, w_gate_up, w_down, final_norm_g):
    b, s, d = x.shape
    depth = w_in.shape[0]
    tm = min(512, s)
    tq = min(512, s)
    th = 512

    perm, n_in = _in_proj_perm()
    assert n_in == w_in.shape[-1]
    win_p = _take_cols(w_in, perm).astype(BF16)
    uq_perm = np.concatenate([np.concatenate([np.arange(h * MLA_QK_DIM, (h + 1) * MLA_QK_DIM),
                                              -np.ones(MLA_QK_PAD - MLA_QK_DIM, np.int64)])
                              for h in range(MLA_HEADS)]).astype(np.int32)
    wuq_p = _take_cols(w_uq, uq_perm).astype(BF16)
    hw = MLA_NOPE_DIM + MLA_V_DIM
    ukv_perm = np.concatenate([np.arange(h * hw, h * hw + MLA_NOPE_DIM) for h in range(MLA_HEADS)]
                              + [np.arange(h * hw + MLA_NOPE_DIM, (h + 1) * hw) for h in range(MLA_HEADS)])
    wukv_p = _take_cols(w_ukv, ukv_perm).astype(BF16)
    conv_perm = np.concatenate([np.concatenate([np.arange(g * SSD_GINNER, (g + 1) * SSD_GINNER),
                                                SSD_INNER + np.arange(g * SSD_STATE, (g + 1) * SSD_STATE),
                                                SSD_INNER + SSD_GROUPS * SSD_STATE
                                                + np.arange(g * SSD_STATE, (g + 1) * SSD_STATE)])
                                for g in range(SSD_GROUPS)]).astype(np.int32)
    cw_p = _take_cols(conv_w, conv_perm).reshape(depth, SSD_CONV, SSD_GROUPS, SSD_GCONV)
    cw_p = cw_p.transpose(0, 2, 1, 3)
    cb_p = _take_cols(conv_b, conv_perm).reshape(depth, SSD_GROUPS, 1, SSD_GCONV)
    dtb_p = _group_heads(dt_bias)
    alog_p = _group_heads(a_log)
    dskip_p = jnp.repeat(d_skip, SSD_HEAD_DIM, axis=-1).reshape(depth, SSD_GROUPS, 1, SSD_GINNER)
    ng_p = ssd_norm_g.reshape(depth, SSD_GROUPS, 1, SSD_GINNER)
    wout_b = w_out.astype(BF16)
    wgu_b = w_gate_up.astype(BF16)
    wd_b = w_down.astype(BF16)
    cosa, sina = _rope_tables(s, GQA_HEAD_DIM)
    cosb, sinb = _rope_tables(s, MLA_ROPE_DIM)

    mod = _modulation(c, w_ada, b_ada).reshape(depth, b, 6, d)
    fg = final_norm_g.reshape(1, d)

    for l in range(depth):
        xbc, z, dt, qa, ka, va, qb, kb, vb = _prep(
            x, mod[l], norm1_g[l].reshape(1, d), win_p[l], q_norm_g[l].reshape(1, -1), k_norm_g[l].reshape(1, -1),
            cosa, sina, mla_q_norm_g[l].reshape(1, -1), wuq_p[l], mla_kv_norm_g[l].reshape(1, -1), wukv_p[l],
            cosb, sinb, tm)
        oa = _attention(qa, ka, va, groups=GQA_KV_HEADS, heads=GQA_HEADS // GQA_KV_HEADS, shared_kv=True,
                        dk=GQA_HEAD_DIM, dv=GQA_HEAD_DIM, tq=tq)
        ob = _attention(qb, kb, vb, groups=MLA_HEADS // 2, heads=2, shared_kv=False,
                        dk=MLA_QK_PAD, dv=MLA_V_DIM, tq=tq)
        oc = _ssd(xbc, z, dt, cw_p[l], cb_p[l], dtb_p[l], alog_p[l], dskip_p[l], ng_p[l])
        x = _outproj(x, mod[l], oa, ob, oc, wout_b[l], tm)
        x = _ffn(x, mod[l], norm2_g[l].reshape(1, d), wgu_b[l], wd_b[l], fg, tm, th, final=(l == depth - 1))
    return x
```

```python
import functools

import numpy as np
import jax
import jax.numpy as jnp
from jax import lax
from jax.experimental import pallas as pl
from jax.experimental.pallas import tpu as pltpu

F32 = jnp.float32
BF16 = jnp.bfloat16

GRID_W = 64
ROPE_THETA = 10000.0
EPS = 1e-6
LOG2E = 1.4426950408889634

GQA_HEADS = 6
GQA_KV_HEADS = 2
GQA_HEAD_DIM = 128
GQA_WIDTH = GQA_HEADS * GQA_HEAD_DIM
GQA_KV_WIDTH = GQA_KV_HEADS * GQA_HEAD_DIM

MLA_HEADS = 4
MLA_Q_LORA = 512
MLA_KV_LORA = 256
MLA_NOPE_DIM = 128
MLA_ROPE_DIM = 64
MLA_V_DIM = 128
MLA_QK_DIM = MLA_NOPE_DIM + MLA_ROPE_DIM
MLA_QK_PAD = 256
MLA_WIDTH = MLA_HEADS * MLA_V_DIM

SSD_HEADS = 12
SSD_HEAD_DIM = 64
SSD_GROUPS = 2
SSD_STATE = 128
SSD_CONV = 5
SSD_CHUNK = 128
SSD_INNER = SSD_HEADS * SSD_HEAD_DIM
SSD_CONV_DIM = SSD_INNER + 2 * SSD_GROUPS * SSD_STATE
SSD_GHEADS = SSD_HEADS // SSD_GROUPS
SSD_GINNER = SSD_INNER // SSD_GROUPS
SSD_GCONV = SSD_GINNER + 2 * SSD_STATE
SSD_HEAD_ROWS = 16

LANES = 128
CONV_PAD_ROWS = 16
ATTN_KEY_CHUNK = 512

SEG_XBC = (0, SSD_GROUPS * SSD_GCONV)
SEG_Z = (SEG_XBC[1], SEG_XBC[1] + SSD_INNER)
SEG_DT = (SEG_Z[1], SEG_Z[1] + SSD_GROUPS * LANES)
SEG_QA = (SEG_DT[1], SEG_DT[1] + GQA_WIDTH)
SEG_KA = (SEG_QA[1], SEG_QA[1] + GQA_KV_WIDTH)
SEG_VA = (SEG_KA[1], SEG_KA[1] + GQA_KV_WIDTH)
SEG_CQ = (SEG_VA[1], SEG_VA[1] + MLA_Q_LORA)
SEG_CKV = (SEG_CQ[1], SEG_CQ[1] + MLA_KV_LORA)
SEG_KPE = (SEG_CKV[1], SEG_CKV[1] + LANES)
IN_COLS_PAD = SEG_KPE[1]

VMEM_LIMIT = 56 * 1024 * 1024


def _cparams(sem):
    return pltpu.CompilerParams(dimension_semantics=sem, vmem_limit_bytes=VMEM_LIMIT)


def _dot(a, b):
    return jnp.dot(a, b, preferred_element_type=F32)


def _dot_nt(a, b):
    return lax.dot_general(a, b, (((1,), (1,)), ((), ())), preferred_element_type=F32)


def _rms(x, g):
    return x * lax.rsqrt(jnp.mean(x * x, axis=-1, keepdims=True) + EPS) * g


def _silu(x):
    return x * (1.0 / (1.0 + jnp.exp(-x)))


def _swap_quarters(x, q):
    lane = lax.broadcasted_iota(jnp.int32, x.shape, 1)
    even = (lane // q) % 2 == 0
    return jnp.where(even, pltpu.roll(x, LANES - q, 1), pltpu.roll(x, q, 1))


def _rope(x, cos, sin, q):
    return x * cos + _swap_quarters(x, q) * sin


def _mod_kernel(c_ref, w_ref, b_ref, o_ref):
    c = c_ref[...]
    act = _silu(c).astype(BF16)
    o_ref[...] = _dot(act, w_ref[...].astype(BF16)) + b_ref[...]


def _modulation(c, w_ada, b_ada):
    depth, d, n = w_ada.shape
    b = c.shape[0]
    tn = 1024
    return pl.pallas_call(
        _mod_kernel,
        grid=(depth, n // tn),
        in_specs=[
            pl.BlockSpec((b, d), lambda l, j: (0, 0)),
            pl.BlockSpec((None, d, tn), lambda l, j: (l, 0, j)),
            pl.BlockSpec((None, 1, tn), lambda l, j: (l, 0, j)),
        ],
        out_specs=pl.BlockSpec((None, b, tn), lambda l, j: (l, 0, j)),
        out_shape=jax.ShapeDtypeStruct((depth, b, n), F32),
        compiler_params=_cparams(("arbitrary", "arbitrary")),
        name="modulation",
    )(c, w_ada, b_ada.reshape(depth, 1, n))


def _prep_kernel(x_ref, mod_ref, g1_ref, win_ref, qg_ref, kg_ref, cosa_ref, sina_ref,
                 mqg_ref, wuq_ref, mkvg_ref, wukv_ref, cosb_ref, sinb_ref,
                 xbc_ref, z_ref, dt_ref, qat_ref, ka_ref, vat_ref, qbt_ref, kb_ref, vbt_ref):
    x = x_ref[...]
    h = (_rms(x, g1_ref[...]) * (1.0 + mod_ref[1:2, :]) + mod_ref[0:1, :]).astype(BF16)

    def proj(seg):
        return _dot(h, win_ref[:, seg[0]:seg[1]])

    def put_t(ref, blk, val):
        ref[blk * LANES:(blk + 1) * LANES, :] = val.T.astype(BF16)

    cosa = cosa_ref[...]
    sina = sina_ref[...]
    cosb = cosb_ref[...]
    sinb = sinb_ref[...]
    qscale = GQA_HEAD_DIM ** -0.5 * LOG2E
    bscale = MLA_QK_DIM ** -0.5 * LOG2E

    qa = proj(SEG_QA)
    ka = proj(SEG_KA)
    for hd in range(GQA_HEADS):
        sl = slice(hd * GQA_HEAD_DIM, (hd + 1) * GQA_HEAD_DIM)
        q = _rope(_rms(qa[:, sl], qg_ref[...]), cosa, sina, GQA_HEAD_DIM // 4)
        put_t(qat_ref, hd, q * qscale)
    cq_raw = proj(SEG_CQ)
    for hd in range(GQA_KV_HEADS):
        sl = slice(hd * GQA_HEAD_DIM, (hd + 1) * GQA_HEAD_DIM)
        k = _rope(_rms(ka[:, sl], kg_ref[...]), cosa, sina, GQA_HEAD_DIM // 4)
        ka_ref[:, sl] = k.astype(BF16)
    xbc = proj(SEG_XBC)
    cq = _rms(cq_raw, mqg_ref[...]).astype(BF16)
    qb = _dot(cq, wuq_ref[...])
    xbc_ref[...] = xbc.astype(BF16)
    ckv_raw = proj(SEG_CKV)
    kpe_raw = proj(SEG_KPE)
    for hd in range(MLA_HEADS):
        c0 = hd * MLA_QK_PAD
        put_t(qbt_ref, 2 * hd, qb[:, c0:c0 + LANES] * bscale)
        qpe = _rope(qb[:, c0 + LANES:c0 + 2 * LANES], cosb, sinb, MLA_ROPE_DIM // 4)
        put_t(qbt_ref, 2 * hd + 1, qpe * bscale)
    z = proj(SEG_Z)
    ckv = _rms(ckv_raw, mkvg_ref[...]).astype(BF16)
    kv = _dot(ckv, wukv_ref[...])
    z_ref[...] = z.astype(BF16)
    kpe = _rope(kpe_raw, cosb, sinb, MLA_ROPE_DIM // 4).astype(BF16)
    va = proj(SEG_VA)
    dt = proj(SEG_DT)
    for hd in range(MLA_HEADS):
        c0 = hd * MLA_QK_PAD
        kb_ref[:, c0:c0 + LANES] = kv[:, hd * LANES:(hd + 1) * LANES].astype(BF16)
        kb_ref[:, c0 + LANES:c0 + 2 * LANES] = kpe
        put_t(vbt_ref, hd, kv[:, (MLA_HEADS + hd) * LANES:(MLA_HEADS + hd + 1) * LANES])
    for hd in range(GQA_KV_HEADS):
        put_t(vat_ref, hd, va[:, hd * GQA_HEAD_DIM:(hd + 1) * GQA_HEAD_DIM])
    dt_ref[...] = dt


def _layer_spec(a, l, single=False):
    nd = a.ndim - 1
    return pl.BlockSpec((None,) + a.shape[1:], lambda *_: (l,) + (0,) * nd,
                        pipeline_mode=pl.Buffered(1) if single else None)


def _mod_spec(mod, l):
    return pl.BlockSpec((None, None) + mod.shape[2:], lambda bi, *_: (l, bi, 0, 0))


def _prep(x, mod, g1, win, qg, kg, cosa, sina, mqg, wuq, mkvg, wukv, cosb, sinb, l, tm):
    b, s, d = x.shape
    row = lambda w: pl.BlockSpec((None, tm, w), lambda bi, i: (bi, i, 0))
    tab = pl.BlockSpec((tm, LANES), lambda bi, i: (i, 0))
    col = lambda w: pl.BlockSpec((None, w, tm), lambda bi, i: (bi, 0, i))
    outs = ((SSD_CONV_DIM, BF16, False), (SSD_INNER, BF16, False), (SSD_GROUPS * LANES, F32, False),
            (GQA_WIDTH, BF16, True), (GQA_KV_WIDTH, BF16, False), (GQA_KV_WIDTH, BF16, True),
            (MLA_HEADS * MLA_QK_PAD, BF16, True), (MLA_HEADS * MLA_QK_PAD, BF16, False), (MLA_WIDTH, BF16, True))
    big = lambda a: _layer_spec(a, l, single=True)
    small = lambda a: _layer_spec(a, l)
    return pl.pallas_call(
        _prep_kernel,
        grid=(b, s // tm),
        in_specs=[
            row(d), _mod_spec(mod, l),
            small(g1), big(win), small(qg), small(kg), tab, tab,
            small(mqg), big(wuq), small(mkvg), big(wukv), tab, tab,
        ],
        out_specs=[col(w) if t else row(w) for w, _, t in outs],
        out_shape=[jax.ShapeDtypeStruct((b, w, s) if t else (b, s, w), dt) for w, dt, t in outs],
        compiler_params=_cparams(("arbitrary", "arbitrary")),
        name="prep",
    )(x, mod, g1, win, qg, kg, cosa, sina, mqg, wuq, mkvg, wukv, cosb, sinb)


def _attn_kernel(qt_ref, k_ref, vt_ref, o_ref, *, heads, shared_kv, dk, dv):
    s_len = k_ref.shape[0]
    kc = min(ATTN_KEY_CHUNK, s_len)
    nchunk = s_len // kc
    stages = [(r, c) for r in range(heads) for c in range(nchunk)]

    def scores(r, c):
        kv = 0 if shared_kv else r
        return _dot(k_ref[c * kc:(c + 1) * kc, kv * dk:(kv + 1) * dk], qt_ref[r * dk:(r + 1) * dk, :])

    st_next = scores(*stages[0])
    for i, (r, c) in enumerate(stages):
        st = st_next
        if i + 1 < len(stages):
            st_next = scores(*stages[i + 1])
        kv = 0 if shared_kv else r
        vt = vt_ref[kv * dv:(kv + 1) * dv, c * kc:(c + 1) * kc]
        mc = jnp.max(st, axis=0, keepdims=True)
        if c == 0:
            m = mc
            p = jnp.exp2(st - m)
            l = jnp.sum(p, axis=0, keepdims=True)
            acc = _dot(vt, p.astype(BF16))
        else:
            m_new = jnp.maximum(m, mc)
            alpha = jnp.exp2(m - m_new)
            p = jnp.exp2(st - m_new)
            l = alpha * l + jnp.sum(p, axis=0, keepdims=True)
            acc = alpha * acc + _dot(vt, p.astype(BF16))
            m = m_new
        if c == nchunk - 1:
            o_ref[:, r * dv:(r + 1) * dv] = (acc * (1.0 / l)).T.astype(o_ref.dtype)


def _attention(qt, k, vt, *, groups, heads, shared_kv, dk, dv, tq):
    b, _, s = qt.shape
    nkv = 1 if shared_kv else heads
    return pl.pallas_call(
        functools.partial(_attn_kernel, heads=heads, shared_kv=shared_kv, dk=dk, dv=dv),
        grid=(b, groups, s // tq),
        in_specs=[
            pl.BlockSpec((None, heads * dk, tq), lambda bi, g, i: (bi, g, i)),
            pl.BlockSpec((None, s, nkv * dk), lambda bi, g, i: (bi, 0, g)),
            pl.BlockSpec((None, nkv * dv, s), lambda bi, g, i: (bi, g, 0)),
        ],
        out_specs=pl.BlockSpec((None, tq, heads * dv), lambda bi, g, i: (bi, i, g)),
        out_shape=jax.ShapeDtypeStruct((b, s, groups * heads * dv), BF16),
        compiler_params=_cparams(("arbitrary", "arbitrary", "arbitrary")),
        name="attention",
    )(qt, k, vt)


def _split3(a):
    hi = a.astype(BF16)
    r = a - hi.astype(F32)
    mid = r.astype(BF16)
    lo = (r - mid.astype(F32)).astype(BF16)
    return hi, mid, lo


def _head_rows(a, lo_row, width):
    return jnp.concatenate([jnp.broadcast_to(a[lo_row + h:lo_row + h + 1, :], (SSD_HEAD_DIM, width))
                            for h in range(SSD_GHEADS)], axis=0)


def _ssd_kernel(xbc_ref, z_ref, dt_ref, cw_ref, cb_ref, dtb_ref, alogr_ref, alogc_ref, dskip_ref, ng_ref, o_ref,
                xpad_ref, xst_ref, bm_ref, cm_ref, posc_ref, dtt_ref, cst_ref, yt_ref, hf_ref, hb_ref):
    s = xbc_ref.shape[0]
    nc = s // SSD_CHUNK
    L = SSD_CHUNK
    G = SSD_GINNER
    HR = SSD_HEAD_ROWS

    row = lax.broadcasted_iota(jnp.int32, (L, L), 0)
    col = lax.broadcasted_iota(jnp.int32, (L, L), 1)
    s_le_t = row <= col
    s_ge_t = row >= col
    tri = s_ge_t.astype(BF16)
    is_fwd_lane = lax.broadcasted_iota(jnp.int32, (L, LANES), 1) < SSD_GHEADS
    is_fwd_row = lax.broadcasted_iota(jnp.int32, (HR, L), 0) < SSD_GHEADS
    dtb = dtb_ref[...]
    aneg_row = -jnp.exp(alogr_ref[...])
    aneg_col = -jnp.exp(alogc_ref[...])

    zeros_pad = jnp.zeros((CONV_PAD_ROWS, SSD_GCONV), BF16)
    xpad_ref[0:CONV_PAD_ROWS, :] = zeros_pad
    xpad_ref[CONV_PAD_ROWS + s:2 * CONV_PAD_ROWS + s, :] = zeros_pad
    xpad_ref[CONV_PAD_ROWS:CONV_PAD_ROWS + s, :] = xbc_ref[...]
    cw = cw_ref[...]
    cb = cb_ref[...]
    half = SSD_CONV // 2
    wrow = lax.broadcasted_iota(jnp.int32, (L, L + 2 * CONV_PAD_ROWS), 0)
    wcol = lax.broadcasted_iota(jnp.int32, (L, L + 2 * CONV_PAD_ROWS), 1)
    shifts = [(wcol == wrow + (CONV_PAD_ROWS - half + j)).astype(BF16) for j in range(SSD_CONV)]

    def conv_body(c, carry):
        base = pl.multiple_of(c * L, L)
        win = xpad_ref[pl.ds(base, L + 2 * CONV_PAD_ROWS), :]
        acc = cb
        for j in range(SSD_CONV):
            if j == half:
                tap = xbc_ref[pl.ds(base, L), :].astype(F32)
            else:
                tap = _dot(shifts[j], win)
            acc = acc + tap * cw[j:j + 1, :]
        act = _silu(acc)
        for k in range(G // LANES):
            xst_ref[c, k * LANES:(k + 1) * LANES, :] = act[:, k * LANES:(k + 1) * LANES].T
        bm_ref[c] = act[:, G:G + SSD_STATE].astype(BF16)
        cm_ref[c] = act[:, G + SSD_STATE:].astype(BF16)
        dtr = dt_ref[pl.ds(base, L), :] + dtb
        dtv = jnp.maximum(dtr, 0.0) + jnp.log1p(jnp.exp(-jnp.abs(dtr)))
        a = dtv * aneg_row
        hi, mid, lo = _split3(a)
        cs = _dot(tri, hi) + _dot(tri, mid) + _dot(tri, lo)
        posc_ref[c] = jnp.where(is_fwd_lane, cs, cs - a)
        dtt_ref[c] = dtv.T[0:HR, :]
        cst_ref[c] = cs.T[0:HR, :]
        return carry

    lax.fori_loop(0, nc, conv_body, 0)

    dskip = dskip_ref[...]
    hf_ref[...] = jnp.zeros_like(hf_ref)
    hb_ref[...] = jnp.zeros_like(hb_ref)
    neg_big = -1e30
    zero_half = jnp.zeros((SSD_HEAD_DIM, L), F32)

    def direction(ci, h_ref, lo, fwd):
        xst = xst_ref[ci]
        bmc = bm_ref[ci]
        cmc = cm_ref[ci]
        posc = posc_ref[ci]
        dtt = dtt_ref[ci]
        cst = cst_ref[ci]
        tot = jnp.broadcast_to(cst[:, L - 1:L], (HR, L))
        ex = cst - dtt * aneg_col
        post = jnp.where(is_fwd_row, cst, ex)
        d_in = jnp.exp(jnp.where(is_fwd_row, cst, tot - ex))
        d_out = jnp.exp(jnp.where(is_fwd_row, tot - cst, ex))
        cdec = jnp.exp(tot)
        cbt = _dot_nt(bmc, cmc)
        ys = []
        for p in range(SSD_GHEADS // 2):
            ws = []
            lhs = []
            for hh in range(2):
                hr = lo + 2 * p + hh
                pc = posc[:, hr:hr + 1]
                pr = post[hr:hr + 1, :]
                if fwd:
                    seg = jnp.where(s_le_t, pr - pc, neg_big)
                else:
                    seg = jnp.where(s_ge_t, pc - pr, neg_big)
                ws.append((cbt * jnp.exp(seg)).astype(BF16))
                dt_rows = jnp.broadcast_to(dtt[hr:hr + 1, :], (SSD_HEAD_DIM, L))
                sel = [zero_half, zero_half]
                sel[hh] = dt_rows
                lhs.append(xst[p * LANES:(p + 1) * LANES, :] * jnp.concatenate(sel, axis=0))
            ys.append(_dot(jnp.concatenate(lhs, axis=1).astype(BF16), jnp.concatenate(ws, axis=0)))
        y = jnp.concatenate(ys, axis=0)
        hprev = h_ref[...]
        y = y + _dot_nt(hprev.astype(BF16), cmc) * _head_rows(d_in, lo, L)
        xo = (xst * _head_rows(dtt * d_out, lo, L)).astype(BF16)
        h_ref[...] = hprev * _head_rows(cdec, lo, SSD_STATE) + _dot(xo, bmc)
        return y, xst

    def scan_body(c, carry):
        yf, xst = direction(c, hf_ref, 0, True)
        yt_ref[c] = yt_ref[c] + yf + xst * dskip
        cb_ = nc - 1 - c
        yb, _ = direction(cb_, hb_ref, SSD_GHEADS, False)
        yt_ref[cb_] = yt_ref[cb_] + yb
        return carry

    yt_ref[...] = jnp.zeros_like(yt_ref)
    lax.fori_loop(0, nc, scan_body, 0)

    ng = ng_ref[...]

    def out_body(c, carry):
        base = pl.multiple_of(c * L, L)
        yt = yt_ref[c]
        y = jnp.concatenate([yt[k * LANES:(k + 1) * LANES, :].T for k in range(G // LANES)], axis=1)
        y = y * _silu(z_ref[pl.ds(base, L), :].astype(F32))
        o_ref[pl.ds(base, L), :] = _rms(y, ng).astype(o_ref.dtype)
        return carry

    lax.fori_loop(0, nc, out_body, 0)


def _ssd(xbc, z, dt, cw, cb, dtb, alogr, alogc, dskip, ng):
    b, s, _ = xbc.shape
    nc = s // SSD_CHUNK
    par = lambda a: pl.BlockSpec((None,) + a.shape[1:], lambda bi, g: (g, 0, 0))
    return pl.pallas_call(
        _ssd_kernel,
        grid=(b, SSD_GROUPS),
        in_specs=[
            pl.BlockSpec((None, s, SSD_GCONV), lambda bi, g: (bi, 0, g)),
            pl.BlockSpec((None, s, SSD_GINNER), lambda bi, g: (bi, 0, g)),
            pl.BlockSpec((None, s, LANES), lambda bi, g: (bi, 0, g)),
            par(cw), par(cb), par(dtb), par(alogr), par(alogc), par(dskip), par(ng),
        ],
        out_specs=pl.BlockSpec((None, s, SSD_GINNER), lambda bi, g: (bi, 0, g)),
        out_shape=jax.ShapeDtypeStruct((b, s, SSD_INNER), BF16),
        scratch_shapes=[
            pltpu.VMEM((s + 2 * CONV_PAD_ROWS, SSD_GCONV), BF16),
            pltpu.VMEM((nc, SSD_GINNER, SSD_CHUNK), F32),
            pltpu.VMEM((nc, SSD_CHUNK, SSD_STATE), BF16),
            pltpu.VMEM((nc, SSD_CHUNK, SSD_STATE), BF16),
            pltpu.VMEM((nc, SSD_CHUNK, LANES), F32),
            pltpu.VMEM((nc, SSD_HEAD_ROWS, SSD_CHUNK), F32),
            pltpu.VMEM((nc, SSD_HEAD_ROWS, SSD_CHUNK), F32),
            pltpu.VMEM((nc, SSD_GINNER, SSD_CHUNK), F32),
            pltpu.VMEM((SSD_GINNER, SSD_STATE), F32),
            pltpu.VMEM((SSD_GINNER, SSD_STATE), F32),
        ],
        compiler_params=_cparams(("arbitrary", "arbitrary")),
        name="ssd",
    )(xbc, z, dt, cw, cb, dtb, alogr, alogc, dskip, ng)


def _outproj_kernel(x_ref, mod_ref, oa_ref, ob_ref, oc_ref, w_ref, g2_ref, o_ref, h_ref):
    a1 = GQA_WIDTH
    a2 = GQA_WIDTH + MLA_WIDTH
    mix = _dot(oa_ref[...], w_ref[0:a1, :])
    mix = mix + _dot(ob_ref[...], w_ref[a1:a2, :])
    mix = mix + _dot(oc_ref[...], w_ref[a2:, :])
    x1 = x_ref[...] + mod_ref[2:3, :] * mix
    o_ref[...] = x1
    h_ref[...] = (_rms(x1, g2_ref[...]) * (1.0 + mod_ref[4:5, :]) + mod_ref[3:4, :]).astype(BF16)


def _outproj(x, mod, oa, ob, oc, w, g2, l, tm):
    b, s, d = x.shape
    row = lambda wd: pl.BlockSpec((None, tm, wd), lambda bi, i: (bi, i, 0))
    return pl.pallas_call(
        _outproj_kernel,
        grid=(b, s // tm),
        in_specs=[row(d), _mod_spec(mod, l), row(oa.shape[-1]), row(ob.shape[-1]), row(oc.shape[-1]),
                  _layer_spec(w, l, single=True), _layer_spec(g2, l)],
        out_specs=[row(d), row(d)],
        out_shape=[jax.ShapeDtypeStruct(x.shape, F32), jax.ShapeDtypeStruct(x.shape, BF16)],
        compiler_params=_cparams(("arbitrary", "arbitrary")),
        name="outproj",
    )(x, mod, oa, ob, oc, w, g2)


def _ffn_kernel(x_ref, h_ref, mod_ref, wg_ref, wu_ref, wd_ref, fg_ref, o_ref, *, final):
    j = pl.program_id(2)

    @pl.when(j == 0)
    def _():
        o_ref[...] = x_ref[...]

    h = h_ref[...]
    part = _dot((_silu(_dot(h, wg_ref[...])) * _dot(h, wu_ref[...])).astype(BF16), wd_ref[...])
    o_ref[...] += mod_ref[5:6, :] * part

    if final:
        @pl.when(j == pl.num_programs(2) - 1)
        def _():
            o_ref[...] = _rms(o_ref[...], fg_ref[...])


def _ffn(x, h, mod, wgu, wd, fg, l, tm, th, final):
    b, s, d = x.shape
    hidden = wd.shape[1]
    nh = hidden // th
    row = pl.BlockSpec((None, tm, d), lambda bi, i, j: (bi, i, 0))
    return pl.pallas_call(
        functools.partial(_ffn_kernel, final=final),
        grid=(b, s // tm, nh),
        in_specs=[
            row, row, _mod_spec(mod, l),
            pl.BlockSpec((None, d, th), lambda bi, i, j: (l, 0, j)),
            pl.BlockSpec((None, d, th), lambda bi, i, j: (l, 0, nh + j)),
            pl.BlockSpec((None, th, d), lambda bi, i, j: (l, j, 0)),
            pl.BlockSpec((1, d), lambda bi, i, j: (0, 0)),
        ],
        out_specs=row,
        out_shape=jax.ShapeDtypeStruct(x.shape, F32),
        compiler_params=_cparams(("arbitrary", "arbitrary", "arbitrary")),
        name="ffn",
    )(x, h, mod, wgu, wgu, wd, fg)


def _rope_tables(seq_len, rot_dim):
    rows = seq_len // GRID_W
    row_idx = jnp.repeat(jnp.arange(rows), GRID_W).astype(F32)
    col_idx = jnp.tile(jnp.arange(GRID_W), rows).astype(F32)
    axis_dim = rot_dim // 2
    inv_freq = jnp.power(ROPE_THETA, -jnp.arange(0, axis_dim, 2, dtype=F32) / axis_dim)
    ang_r = row_idx[:, None] * inv_freq[None, :]
    ang_c = col_idx[:, None] * inv_freq[None, :]
    cos = jnp.concatenate([jnp.cos(ang_r)] * 2 + [jnp.cos(ang_c)] * 2, axis=-1)
    sin = jnp.concatenate([-jnp.sin(ang_r), jnp.sin(ang_r), -jnp.sin(ang_c), jnp.sin(ang_c)], axis=-1)
    pad = LANES - rot_dim
    if pad:
        cos = jnp.pad(cos, ((0, 0), (0, pad)))
        sin = jnp.pad(sin, ((0, 0), (0, pad)))
    return cos, sin


def _in_proj_perm():
    o_q = 0
    o_k = o_q + GQA_WIDTH
    o_v = o_k + GQA_KV_WIDTH
    o_cq = o_v + GQA_KV_WIDTH
    o_ckv = o_cq + MLA_Q_LORA
    o_kpe = o_ckv + MLA_KV_LORA
    o_z = o_kpe + MLA_ROPE_DIM
    o_xs = o_z + SSD_INNER
    o_bm = o_xs + SSD_INNER
    o_cm = o_bm + SSD_GROUPS * SSD_STATE
    o_dt = o_cm + SSD_GROUPS * SSD_STATE
    ar = lambda a, n: list(range(a, a + n))
    perm = []
    for g in range(SSD_GROUPS):
        perm += ar(o_xs + g * SSD_GINNER, SSD_GINNER) + ar(o_bm + g * SSD_STATE, SSD_STATE)
        perm += ar(o_cm + g * SSD_STATE, SSD_STATE)
    perm += ar(o_z, SSD_INNER)
    for g in range(SSD_GROUPS):
        perm += ar(o_dt + g * SSD_GHEADS, SSD_GHEADS) + ar(o_dt + SSD_HEADS + g * SSD_GHEADS, SSD_GHEADS)
        perm += [-1] * (LANES - 2 * SSD_GHEADS)
    perm += ar(o_q, GQA_WIDTH) + ar(o_k, GQA_KV_WIDTH) + ar(o_v, GQA_KV_WIDTH)
    perm += ar(o_cq, MLA_Q_LORA) + ar(o_ckv, MLA_KV_LORA)
    perm += ar(o_kpe, MLA_ROPE_DIM) + [-1] * (LANES - MLA_ROPE_DIM)
    assert len(perm) == IN_COLS_PAD
    return np.asarray(perm, np.int32), o_dt + 2 * SSD_HEADS


def _take_cols(w, perm):
    parts = []
    i = 0
    while i < len(perm):
        j = i + 1
        if perm[i] < 0:
            while j < len(perm) and perm[j] < 0:
                j += 1
            parts.append(jnp.zeros(w.shape[:-1] + (j - i,), w.dtype))
        else:
            while j < len(perm) and perm[j] == perm[j - 1] + 1:
                j += 1
            parts.append(w[..., int(perm[i]):int(perm[i]) + (j - i)])
        i = j
    return jnp.concatenate(parts, axis=-1)


def _group_heads(a):
    depth = a.shape[0]
    a = a.reshape(depth, 2, SSD_GROUPS, SSD_GHEADS).transpose(0, 2, 1, 3).reshape(depth, SSD_GROUPS, 2 * SSD_GHEADS)
    return jnp.pad(a, ((0, 0), (0, 0), (0, LANES - 2 * SSD_GHEADS)))[:, :, None, :]


def kernel(x, c, w_ada, b_ada, norm1_g, norm2_g, w_in, q_norm_g, k_norm_g, mla_q_norm_g, w_uq, mla_kv_norm_g,
           w_ukv, conv_w, conv_b, dt_bias, a_log, d_skip, ssd_norm_g, w_out, w_gate_up, w_down, final_norm_g):
    b, s, d = x.shape
    depth = w_in.shape[0]
    tm = min(512, s)
    tq = min(512, s)
    tm_ffn = min(512, s)
    th = 512

    perm, n_in = _in_proj_perm()
    assert n_in == w_in.shape[-1]
    win_p = _take_cols(w_in, perm).astype(BF16)
    uq_perm = np.concatenate([np.concatenate([np.arange(h * MLA_QK_DIM, (h + 1) * MLA_QK_DIM),
                                              -np.ones(MLA_QK_PAD - MLA_QK_DIM, np.int64)])
                              for h in range(MLA_HEADS)]).astype(np.int32)
    wuq_p = _take_cols(w_uq, uq_perm).astype(BF16)
    hw = MLA_NOPE_DIM + MLA_V_DIM
    ukv_perm = np.concatenate([np.arange(h * hw, h * hw + MLA_NOPE_DIM) for h in range(MLA_HEADS)]
                              + [np.arange(h * hw + MLA_NOPE_DIM, (h + 1) * hw) for h in range(MLA_HEADS)])
    wukv_p = _take_cols(w_ukv, ukv_perm).astype(BF16)
    conv_perm = np.concatenate([np.concatenate([np.arange(g * SSD_GINNER, (g + 1) * SSD_GINNER),
                                                SSD_INNER + np.arange(g * SSD_STATE, (g + 1) * SSD_STATE),
                                                SSD_INNER + SSD_GROUPS * SSD_STATE
                                                + np.arange(g * SSD_STATE, (g + 1) * SSD_STATE)])
                                for g in range(SSD_GROUPS)]).astype(np.int32)
    cw_p = _take_cols(conv_w, conv_perm).reshape(depth, SSD_CONV, SSD_GROUPS, SSD_GCONV)
    cw_p = cw_p.transpose(0, 2, 1, 3)
    cb_p = _take_cols(conv_b, conv_perm).reshape(depth, SSD_GROUPS, 1, SSD_GCONV)
    dtb_p = _group_heads(dt_bias)
    alogr_p = _group_heads(a_log)
    alogc_p = jnp.broadcast_to(alogr_p[:, :, 0, :SSD_HEAD_ROWS, None], (depth, SSD_GROUPS, SSD_HEAD_ROWS, LANES))
    dskip_p = jnp.broadcast_to(jnp.repeat(d_skip, SSD_HEAD_DIM, axis=-1).reshape(depth, SSD_GROUPS, SSD_GINNER, 1),
                               (depth, SSD_GROUPS, SSD_GINNER, LANES))
    ng_p = ssd_norm_g.reshape(depth, SSD_GROUPS, 1, SSD_GINNER)
    wout_b = w_out.astype(BF16)
    wgu_b = w_gate_up.astype(BF16)
    wd_b = w_down.astype(BF16)
    cosa, sina = _rope_tables(s, GQA_HEAD_DIM)
    cosb, sinb = _rope_tables(s, MLA_ROPE_DIM)

    mod = _modulation(c, w_ada, b_ada).reshape(depth, b, 6, d)
    fg = final_norm_g.reshape(1, d)

    vec = lambda a: a.reshape(depth, 1, -1)
    for l in range(depth):
        xbc, z, dt, qa, ka, va, qb, kb, vb = _prep(
            x, mod, vec(norm1_g), win_p, vec(q_norm_g), vec(k_norm_g), cosa, sina,
            vec(mla_q_norm_g), wuq_p, vec(mla_kv_norm_g), wukv_p, cosb, sinb, l, tm)
        oa = _attention(qa, ka, va, groups=GQA_KV_HEADS, heads=GQA_HEADS // GQA_KV_HEADS, shared_kv=True,
                        dk=GQA_HEAD_DIM, dv=GQA_HEAD_DIM, tq=tq)
        ob = _attention(qb, kb, vb, groups=MLA_HEADS // 2, heads=2, shared_kv=False,
                        dk=MLA_QK_PAD, dv=MLA_V_DIM, tq=tq)
        oc = _ssd(xbc, z, dt, cw_p[l], cb_p[l], dtb_p[l], alogr_p[l], alogc_p[l], dskip_p[l], ng_p[l])
        x, h2 = _outproj(x, mod, oa, ob, oc, wout_b, vec(norm2_g), l, tm)
        x = _ffn(x, h2, mod, wgu_b, wd_b, fg, l, tm_ffn, th, final=(l == depth - 1))
    return x
```

```python
import functools

import numpy as np
import jax
import jax.numpy as jnp
from jax import lax
from jax.experimental import pallas as pl
from jax.experimental.pallas import tpu as pltpu

F32 = jnp.float32
BF16 = jnp.bfloat16

GRID_W = 64
ROPE_THETA = 10000.0
EPS = 1e-6
LOG2E = 1.4426950408889634

GQA_HEADS = 6
GQA_KV_HEADS = 2
GQA_HEAD_DIM = 128
GQA_WIDTH = GQA_HEADS * GQA_HEAD_DIM
GQA_KV_WIDTH = GQA_KV_HEADS * GQA_HEAD_DIM

MLA_HEADS = 4
MLA_Q_LORA = 512
MLA_KV_LORA = 256
MLA_NOPE_DIM = 128
MLA_ROPE_DIM = 64
MLA_V_DIM = 128
MLA_QK_DIM = MLA_NOPE_DIM + MLA_ROPE_DIM
MLA_QK_PAD = 256
MLA_WIDTH = MLA_HEADS * MLA_V_DIM

SSD_HEADS = 12
SSD_HEAD_DIM = 64
SSD_GROUPS = 2
SSD_STATE = 128
SSD_CONV = 5
SSD_CHUNK = 128
SSD_INNER = SSD_HEADS * SSD_HEAD_DIM
SSD_CONV_DIM = SSD_INNER + 2 * SSD_GROUPS * SSD_STATE
SSD_GHEADS = SSD_HEADS // SSD_GROUPS
SSD_GINNER = SSD_INNER // SSD_GROUPS
SSD_GCONV = SSD_GINNER + 2 * SSD_STATE
SSD_HEAD_ROWS = 16

LANES = 128
CONV_PAD_ROWS = 16
ATTN_KEY_CHUNK = 512

SEG_XBC = (0, SSD_GROUPS * SSD_GCONV)
SEG_Z = (SEG_XBC[1], SEG_XBC[1] + SSD_INNER)
SEG_DT = (SEG_Z[1], SEG_Z[1] + SSD_GROUPS * LANES)
SEG_QA = (SEG_DT[1], SEG_DT[1] + GQA_WIDTH)
SEG_KA = (SEG_QA[1], SEG_QA[1] + GQA_KV_WIDTH)
SEG_VA = (SEG_KA[1], SEG_KA[1] + GQA_KV_WIDTH)
SEG_CQ = (SEG_VA[1], SEG_VA[1] + MLA_Q_LORA)
SEG_CKV = (SEG_CQ[1], SEG_CQ[1] + MLA_KV_LORA)
SEG_KPE = (SEG_CKV[1], SEG_CKV[1] + LANES)
IN_COLS_PAD = SEG_KPE[1]

VMEM_LIMIT = 56 * 1024 * 1024


def _cparams(sem):
    return pltpu.CompilerParams(dimension_semantics=sem, vmem_limit_bytes=VMEM_LIMIT)


def _dot(a, b):
    return jnp.dot(a, b, preferred_element_type=F32)


def _dot_nt(a, b):
    return lax.dot_general(a, b, (((1,), (1,)), ((), ())), preferred_element_type=F32)


def _rms(x, g):
    return x * lax.rsqrt(jnp.mean(x * x, axis=-1, keepdims=True) + EPS) * g


def _silu(x):
    return x * (1.0 / (1.0 + jnp.exp(-x)))


def _swap_quarters(x, q):
    lane = lax.broadcasted_iota(jnp.int32, x.shape, 1)
    even = (lane // q) % 2 == 0
    return jnp.where(even, pltpu.roll(x, LANES - q, 1), pltpu.roll(x, q, 1))


def _rope(x, cos, sin, q):
    return x * cos + _swap_quarters(x, q) * sin


def _mod_kernel(c_ref, w_ref, b_ref, o_ref):
    c = c_ref[...]
    act = _silu(c).astype(BF16)
    o_ref[...] = _dot(act, w_ref[...].astype(BF16)) + b_ref[...]


def _modulation(c, w_ada, b_ada):
    depth, d, n = w_ada.shape
    b = c.shape[0]
    tn = 1024
    return pl.pallas_call(
        _mod_kernel,
        grid=(depth, n // tn),
        in_specs=[
            pl.BlockSpec((b, d), lambda l, j: (0, 0)),
            pl.BlockSpec((None, d, tn), lambda l, j: (l, 0, j)),
            pl.BlockSpec((None, 1, tn), lambda l, j: (l, 0, j)),
        ],
        out_specs=pl.BlockSpec((None, b, tn), lambda l, j: (l, 0, j)),
        out_shape=jax.ShapeDtypeStruct((depth, b, n), F32),
        compiler_params=_cparams(("arbitrary", "arbitrary")),
        name="modulation",
    )(c, w_ada, b_ada.reshape(depth, 1, n))


def _prep_kernel(x_ref, mod_ref, g1_ref, win_ref, qg_ref, kg_ref, cosa_ref, sina_ref,
                 mqg_ref, wuq_ref, mkvg_ref, wukv_ref, cosb_ref, sinb_ref,
                 xbc_ref, z_ref, dt_ref, qat_ref, ka_ref, vat_ref, qbt_ref, kb_ref, vbt_ref):
    x = x_ref[...]
    h = (_rms(x, g1_ref[...]) * (1.0 + mod_ref[1:2, :]) + mod_ref[0:1, :]).astype(BF16)

    def proj(seg):
        return _dot(h, win_ref[:, seg[0]:seg[1]])

    def put_t(ref, blk, val):
        ref[blk * LANES:(blk + 1) * LANES, :] = val.T.astype(BF16)

    cosa = cosa_ref[...]
    sina = sina_ref[...]
    cosb = cosb_ref[...]
    sinb = sinb_ref[...]
    qscale = GQA_HEAD_DIM ** -0.5 * LOG2E
    bscale = MLA_QK_DIM ** -0.5 * LOG2E

    qa = proj(SEG_QA)
    ka = proj(SEG_KA)
    for hd in range(GQA_HEADS):
        sl = slice(hd * GQA_HEAD_DIM, (hd + 1) * GQA_HEAD_DIM)
        q = _rope(_rms(qa[:, sl], qg_ref[...]), cosa, sina, GQA_HEAD_DIM // 4)
        put_t(qat_ref, hd, q * qscale)
    cq_raw = proj(SEG_CQ)
    for hd in range(GQA_KV_HEADS):
        sl = slice(hd * GQA_HEAD_DIM, (hd + 1) * GQA_HEAD_DIM)
        k = _rope(_rms(ka[:, sl], kg_ref[...]), cosa, sina, GQA_HEAD_DIM // 4)
        ka_ref[:, sl] = k.astype(BF16)
    xbc = proj(SEG_XBC)
    cq = _rms(cq_raw, mqg_ref[...]).astype(BF16)
    qb = _dot(cq, wuq_ref[...])
    xbc_ref[...] = xbc.astype(BF16)
    ckv_raw = proj(SEG_CKV)
    kpe_raw = proj(SEG_KPE)
    for hd in range(MLA_HEADS):
        c0 = hd * MLA_QK_PAD
        put_t(qbt_ref, 2 * hd, qb[:, c0:c0 + LANES] * bscale)
        qpe = _rope(qb[:, c0 + LANES:c0 + 2 * LANES], cosb, sinb, MLA_ROPE_DIM // 4)
        put_t(qbt_ref, 2 * hd + 1, qpe * bscale)
    z = proj(SEG_Z)
    ckv = _rms(ckv_raw, mkvg_ref[...]).astype(BF16)
    kv = _dot(ckv, wukv_ref[...])
    z_ref[...] = z.astype(BF16)
    kpe = _rope(kpe_raw, cosb, sinb, MLA_ROPE_DIM // 4).astype(BF16)
    va = proj(SEG_VA)
    dt = proj(SEG_DT)
    for hd in range(MLA_HEADS):
        c0 = hd * MLA_QK_PAD
        kb_ref[:, c0:c0 + LANES] = kv[:, hd * LANES:(hd + 1) * LANES].astype(BF16)
        kb_ref[:, c0 + LANES:c0 + 2 * LANES] = kpe
        put_t(vbt_ref, hd, kv[:, (MLA_HEADS + hd) * LANES:(MLA_HEADS + hd + 1) * LANES])
    for hd in range(GQA_KV_HEADS):
        put_t(vat_ref, hd, va[:, hd * GQA_HEAD_DIM:(hd + 1) * GQA_HEAD_DIM])
    dt_ref[...] = dt


def _layer_spec(a, l, single=False):
    nd = a.ndim - 1
    return pl.BlockSpec((None,) + a.shape[1:], lambda *_: (l,) + (0,) * nd,
                        pipeline_mode=pl.Buffered(1) if single else None)


def _mod_spec(mod, l):
    return pl.BlockSpec((None, None) + mod.shape[2:], lambda bi, *_: (l, bi, 0, 0))


def _prep(x, mod, g1, win, qg, kg, cosa, sina, mqg, wuq, mkvg, wukv, cosb, sinb, l, tm):
    b, s, d = x.shape
    row = lambda w: pl.BlockSpec((None, tm, w), lambda bi, i: (bi, i, 0))
    tab = pl.BlockSpec((tm, LANES), lambda bi, i: (i, 0))
    col = lambda w: pl.BlockSpec((None, w, tm), lambda bi, i: (bi, 0, i))
    outs = ((SSD_CONV_DIM, BF16, False), (SSD_INNER, BF16, False), (SSD_GROUPS * LANES, F32, False),
            (GQA_WIDTH, BF16, True), (GQA_KV_WIDTH, BF16, False), (GQA_KV_WIDTH, BF16, True),
            (MLA_HEADS * MLA_QK_PAD, BF16, True), (MLA_HEADS * MLA_QK_PAD, BF16, False), (MLA_WIDTH, BF16, True))
    big = lambda a: _layer_spec(a, l, single=True)
    small = lambda a: _layer_spec(a, l)
    return pl.pallas_call(
        _prep_kernel,
        grid=(b, s // tm),
        in_specs=[
            row(d), _mod_spec(mod, l),
            small(g1), big(win), small(qg), small(kg), tab, tab,
            small(mqg), big(wuq), small(mkvg), big(wukv), tab, tab,
        ],
        out_specs=[col(w) if t else row(w) for w, _, t in outs],
        out_shape=[jax.ShapeDtypeStruct((b, w, s) if t else (b, s, w), dt) for w, dt, t in outs],
        compiler_params=_cparams(("arbitrary", "arbitrary")),
        name="prep",
    )(x, mod, g1, win, qg, kg, cosa, sina, mqg, wuq, mkvg, wukv, cosb, sinb)


def _attn_kernel(qt_ref, k_ref, vt_ref, o_ref, *, heads, shared_kv, dk, dv):
    s_len = k_ref.shape[0]
    kc = min(ATTN_KEY_CHUNK, s_len)
    nchunk = s_len // kc
    stages = [(r, c) for r in range(heads) for c in range(nchunk)]

    def scores(r, c):
        kv = 0 if shared_kv else r
        return _dot(k_ref[c * kc:(c + 1) * kc, kv * dk:(kv + 1) * dk], qt_ref[r * dk:(r + 1) * dk, :])

    st_next = scores(*stages[0])
    for i, (r, c) in enumerate(stages):
        st = st_next
        if i + 1 < len(stages):
            st_next = scores(*stages[i + 1])
        kv = 0 if shared_kv else r
        vt = vt_ref[kv * dv:(kv + 1) * dv, c * kc:(c + 1) * kc]
        mc = jnp.max(st, axis=0, keepdims=True)
        if c == 0:
            m = mc
            p = jnp.exp2(st - m)
            l = jnp.sum(p, axis=0, keepdims=True)
            acc = _dot(vt, p.astype(BF16))
        else:
            m_new = jnp.maximum(m, mc)
            alpha = jnp.exp2(m - m_new)
            p = jnp.exp2(st - m_new)
            l = alpha * l + jnp.sum(p, axis=0, keepdims=True)
            acc = alpha * acc + _dot(vt, p.astype(BF16))
            m = m_new
        if c == nchunk - 1:
            o_ref[:, r * dv:(r + 1) * dv] = (acc * (1.0 / l)).T.astype(o_ref.dtype)


def _attention(qt, k, vt, *, groups, heads, shared_kv, dk, dv, tq):
    b, _, s = qt.shape
    nkv = 1 if shared_kv else heads
    return pl.pallas_call(
        functools.partial(_attn_kernel, heads=heads, shared_kv=shared_kv, dk=dk, dv=dv),
        grid=(b, groups, s // tq),
        in_specs=[
            pl.BlockSpec((None, heads * dk, tq), lambda bi, g, i: (bi, g, i)),
            pl.BlockSpec((None, s, nkv * dk), lambda bi, g, i: (bi, 0, g)),
            pl.BlockSpec((None, nkv * dv, s), lambda bi, g, i: (bi, g, 0)),
        ],
        out_specs=pl.BlockSpec((None, tq, heads * dv), lambda bi, g, i: (bi, i, g)),
        out_shape=jax.ShapeDtypeStruct((b, s, groups * heads * dv), BF16),
        compiler_params=_cparams(("arbitrary", "arbitrary", "arbitrary")),
        name="attention",
    )(qt, k, vt)


def _split3(a):
    hi = a.astype(BF16)
    r = a - hi.astype(F32)
    mid = r.astype(BF16)
    lo = (r - mid.astype(F32)).astype(BF16)
    return hi, mid, lo


def _head_rows(a, lo_row, width):
    return jnp.concatenate([jnp.broadcast_to(a[lo_row + h:lo_row + h + 1, :], (SSD_HEAD_DIM, width))
                            for h in range(SSD_GHEADS)], axis=0)


def _ssd_kernel(xbc_ref, z_ref, dt_ref, cw_ref, cb_ref, dtb_ref, alogr_ref, alogc_ref, dskip_ref, ng_ref, o_ref,
                xpad_ref, xst_ref, bm_ref, cm_ref, posc_ref, dtt_ref, cst_ref, yt_ref, hf_ref, hb_ref):
    s = xbc_ref.shape[0]
    nc = s // SSD_CHUNK
    L = SSD_CHUNK
    G = SSD_GINNER
    HR = SSD_HEAD_ROWS

    row = lax.broadcasted_iota(jnp.int32, (L, L), 0)
    col = lax.broadcasted_iota(jnp.int32, (L, L), 1)
    s_le_t = row <= col
    s_ge_t = row >= col
    tri = s_ge_t.astype(BF16)
    is_fwd_lane = lax.broadcasted_iota(jnp.int32, (L, LANES), 1) < SSD_GHEADS
    is_fwd_row = lax.broadcasted_iota(jnp.int32, (HR, L), 0) < SSD_GHEADS
    dtb = dtb_ref[...]
    aneg_row = -jnp.exp(alogr_ref[...])
    aneg_col = -jnp.exp(alogc_ref[...])

    zeros_pad = jnp.zeros((CONV_PAD_ROWS, SSD_GCONV), BF16)
    xpad_ref[0:CONV_PAD_ROWS, :] = zeros_pad
    xpad_ref[CONV_PAD_ROWS + s:2 * CONV_PAD_ROWS + s, :] = zeros_pad
    xpad_ref[CONV_PAD_ROWS:CONV_PAD_ROWS + s, :] = xbc_ref[...]
    cw = cw_ref[...]
    cb = cb_ref[...]
    half = SSD_CONV // 2
    wrow = lax.broadcasted_iota(jnp.int32, (L, L + 2 * CONV_PAD_ROWS), 0)
    wcol = lax.broadcasted_iota(jnp.int32, (L, L + 2 * CONV_PAD_ROWS), 1)
    shifts = [(wcol == wrow + (CONV_PAD_ROWS - half + j)).astype(BF16) for j in range(SSD_CONV)]

    def conv_body(c, carry):
        base = pl.multiple_of(c * L, L)
        win = xpad_ref[pl.ds(base, L + 2 * CONV_PAD_ROWS), :]
        acc = cb
        for j in range(SSD_CONV):
            if j == half:
                tap = xbc_ref[pl.ds(base, L), :].astype(F32)
            else:
                tap = _dot(shifts[j], win)
            acc = acc + tap * cw[j:j + 1, :]
        act = _silu(acc)
        for k in range(G // LANES):
            xst_ref[c, k * LANES:(k + 1) * LANES, :] = act[:, k * LANES:(k + 1) * LANES].T
        bm_ref[c] = act[:, G:G + SSD_STATE].astype(BF16)
        cm_ref[c] = act[:, G + SSD_STATE:].astype(BF16)
        dtr = dt_ref[pl.ds(base, L), :] + dtb
        dtv = jnp.maximum(dtr, 0.0) + jnp.log1p(jnp.exp(-jnp.abs(dtr)))
        a = dtv * aneg_row
        hi, mid, lo = _split3(a)
        cs = _dot(tri, hi) + _dot(tri, mid) + _dot(tri, lo)
        posc_ref[c] = jnp.where(is_fwd_lane, cs, cs - a)
        dtt_ref[c] = dtv.T[0:HR, :]
        cst_ref[c] = cs.T[0:HR, :]
        return carry

    lax.fori_loop(0, nc, conv_body, 0, unroll=4)

    dskip = dskip_ref[...]
    hf_ref[...] = jnp.zeros_like(hf_ref)
    hb_ref[...] = jnp.zeros_like(hb_ref)
    neg_big = -1e30
    zero_half = jnp.zeros((SSD_HEAD_DIM, L), F32)

    def direction(ci, h_ref, lo, fwd):
        xst = xst_ref[ci]
        bmc = bm_ref[ci]
        cmc = cm_ref[ci]
        posc = posc_ref[ci]
        dtt = dtt_ref[ci]
        cst = cst_ref[ci]
        tot = jnp.broadcast_to(cst[:, L - 1:L], (HR, L))
        ex = cst - dtt * aneg_col
        post = jnp.where(is_fwd_row, cst, ex)
        d_in = jnp.exp(jnp.where(is_fwd_row, cst, tot - ex))
        d_out = jnp.exp(jnp.where(is_fwd_row, tot - cst, ex))
        cdec = jnp.exp(tot)
        cbt = _dot_nt(bmc, cmc)
        ys = []
        for p in range(SSD_GHEADS // 2):
            ws = []
            lhs = []
            for hh in range(2):
                hr = lo + 2 * p + hh
                pc = posc[:, hr:hr + 1]
                pr = post[hr:hr + 1, :]
                if fwd:
                    seg = jnp.where(s_le_t, pr - pc, neg_big)
                else:
                    seg = jnp.where(s_ge_t, pc - pr, neg_big)
                ws.append((cbt * jnp.exp(seg)).astype(BF16))
                dt_rows = jnp.broadcast_to(dtt[hr:hr + 1, :], (SSD_HEAD_DIM, L))
                sel = [zero_half, zero_half]
                sel[hh] = dt_rows
                lhs.append(xst[p * LANES:(p + 1) * LANES, :] * jnp.concatenate(sel, axis=0))
            ys.append(_dot(jnp.concatenate(lhs, axis=1).astype(BF16), jnp.concatenate(ws, axis=0)))
        y = jnp.concatenate(ys, axis=0)
        hprev = h_ref[...]
        y = y + _dot_nt(hprev.astype(BF16), cmc) * _head_rows(d_in, lo, L)
        xo = (xst * _head_rows(dtt * d_out, lo, L)).astype(BF16)
        h_ref[...] = hprev * _head_rows(cdec, lo, SSD_STATE) + _dot(xo, bmc)
        return y, xst

    def scan_body(c, carry):
        yf, xst = direction(c, hf_ref, 0, True)
        yt_ref[c] = yt_ref[c] + yf + xst * dskip
        cb_ = nc - 1 - c
        yb, _ = direction(cb_, hb_ref, SSD_GHEADS, False)
        yt_ref[cb_] = yt_ref[cb_] + yb
        return carry

    yt_ref[...] = jnp.zeros_like(yt_ref)
    lax.fori_loop(0, nc, scan_body, 0, unroll=2)

    ng = ng_ref[...]

    def out_body(c, carry):
        base = pl.multiple_of(c * L, L)
        yt = yt_ref[c]
        y = jnp.concatenate([yt[k * LANES:(k + 1) * LANES, :].T for k in range(G // LANES)], axis=1)
        y = y * _silu(z_ref[pl.ds(base, L), :].astype(F32))
        o_ref[pl.ds(base, L), :] = _rms(y, ng).astype(o_ref.dtype)
        return carry

    lax.fori_loop(0, nc, out_body, 0, unroll=4)


def _ssd(xbc, z, dt, cw, cb, dtb, alogr, alogc, dskip, ng):
    b, s, _ = xbc.shape
    nc = s // SSD_CHUNK
    par = lambda a: pl.BlockSpec((None,) + a.shape[1:], lambda bi, g: (g, 0, 0))
    return pl.pallas_call(
        _ssd_kernel,
        grid=(b, SSD_GROUPS),
        in_specs=[
            pl.BlockSpec((None, s, SSD_GCONV), lambda bi, g: (bi, 0, g)),
            pl.BlockSpec((None, s, SSD_GINNER), lambda bi, g: (bi, 0, g)),
            pl.BlockSpec((None, s, LANES), lambda bi, g: (bi, 0, g)),
            par(cw), par(cb), par(dtb), par(alogr), par(alogc), par(dskip), par(ng),
        ],
        out_specs=pl.BlockSpec((None, s, SSD_GINNER), lambda bi, g: (bi, 0, g)),
        out_shape=jax.ShapeDtypeStruct((b, s, SSD_INNER), BF16),
        scratch_shapes=[
            pltpu.VMEM((s + 2 * CONV_PAD_ROWS, SSD_GCONV), BF16),
            pltpu.VMEM((nc, SSD_GINNER, SSD_CHUNK), F32),
            pltpu.VMEM((nc, SSD_CHUNK, SSD_STATE), BF16),
            pltpu.VMEM((nc, SSD_CHUNK, SSD_STATE), BF16),
            pltpu.VMEM((nc, SSD_CHUNK, LANES), F32),
            pltpu.VMEM((nc, SSD_HEAD_ROWS, SSD_CHUNK), F32),
            pltpu.VMEM((nc, SSD_HEAD_ROWS, SSD_CHUNK), F32),
            pltpu.VMEM((nc, SSD_GINNER, SSD_CHUNK), F32),
            pltpu.VMEM((SSD_GINNER, SSD_STATE), F32),
            pltpu.VMEM((SSD_GINNER, SSD_STATE), F32),
        ],
        compiler_params=_cparams(("arbitrary", "arbitrary")),
        name="ssd",
    )(xbc, z, dt, cw, cb, dtb, alogr, alogc, dskip, ng)


def _outproj_kernel(x_ref, mod_ref, oa_ref, ob_ref, oc_ref, w_ref, g2_ref, o_ref, h_ref):
    a1 = GQA_WIDTH
    a2 = GQA_WIDTH + MLA_WIDTH
    mix = _dot(oa_ref[...], w_ref[0:a1, :])
    mix = mix + _dot(ob_ref[...], w_ref[a1:a2, :])
    mix = mix + _dot(oc_ref[...], w_ref[a2:, :])
    x1 = x_ref[...] + mod_ref[2:3, :] * mix
    o_ref[...] = x1
    h_ref[...] = (_rms(x1, g2_ref[...]) * (1.0 + mod_ref[4:5, :]) + mod_ref[3:4, :]).astype(BF16)


def _outproj(x, mod, oa, ob, oc, w, g2, l, tm):
    b, s, d = x.shape
    row = lambda wd: pl.BlockSpec((None, tm, wd), lambda bi, i: (bi, i, 0))
    return pl.pallas_call(
        _outproj_kernel,
        grid=(b, s // tm),
        in_specs=[row(d), _mod_spec(mod, l), row(oa.shape[-1]), row(ob.shape[-1]), row(oc.shape[-1]),
                  _layer_spec(w, l, single=True), _layer_spec(g2, l)],
        out_specs=[row(d), row(d)],
        out_shape=[jax.ShapeDtypeStruct(x.shape, F32), jax.ShapeDtypeStruct(x.shape, BF16)],
        compiler_params=_cparams(("arbitrary", "arbitrary")),
        name="outproj",
    )(x, mod, oa, ob, oc, w, g2)


def _ffn_kernel(x_ref, h_ref, mod_ref, wg_ref, wu_ref, wd_ref, fg_ref, o_ref, *, final):
    j = pl.program_id(2)

    @pl.when(j == 0)
    def _():
        o_ref[...] = x_ref[...]

    h = h_ref[...]
    part = _dot((_silu(_dot(h, wg_ref[...])) * _dot(h, wu_ref[...])).astype(BF16), wd_ref[...])
    o_ref[...] += mod_ref[5:6, :] * part

    if final:
        @pl.when(j == pl.num_programs(2) - 1)
        def _():
            o_ref[...] = _rms(o_ref[...], fg_ref[...])


def _ffn(x, h, mod, wgu, wd, fg, l, tm, th, final):
    b, s, d = x.shape
    hidden = wd.shape[1]
    nh = hidden // th
    row = pl.BlockSpec((None, tm, d), lambda bi, i, j: (bi, i, 0))
    return pl.pallas_call(
        functools.partial(_ffn_kernel, final=final),
        grid=(b, s // tm, nh),
        in_specs=[
            row, row, _mod_spec(mod, l),
            pl.BlockSpec((None, d, th), lambda bi, i, j: (l, 0, j)),
            pl.BlockSpec((None, d, th), lambda bi, i, j: (l, 0, nh + j)),
            pl.BlockSpec((None, th, d), lambda bi, i, j: (l, j, 0)),
            pl.BlockSpec((1, d), lambda bi, i, j: (0, 0)),
        ],
        out_specs=row,
        out_shape=jax.ShapeDtypeStruct(x.shape, F32),
        compiler_params=_cparams(("arbitrary", "arbitrary", "arbitrary")),
        name="ffn",
    )(x, h, mod, wgu, wgu, wd, fg)


def _rope_tables(seq_len, rot_dim):
    rows = seq_len // GRID_W
    row_idx = jnp.repeat(jnp.arange(rows), GRID_W).astype(F32)
    col_idx = jnp.tile(jnp.arange(GRID_W), rows).astype(F32)
    axis_dim = rot_dim // 2
    inv_freq = jnp.power(ROPE_THETA, -jnp.arange(0, axis_dim, 2, dtype=F32) / axis_dim)
    ang_r = row_idx[:, None] * inv_freq[None, :]
    ang_c = col_idx[:, None] * inv_freq[None, :]
    cos = jnp.concatenate([jnp.cos(ang_r)] * 2 + [jnp.cos(ang_c)] * 2, axis=-1)
    sin = jnp.concatenate([-jnp.sin(ang_r), jnp.sin(ang_r), -jnp.sin(ang_c), jnp.sin(ang_c)], axis=-1)
    pad = LANES - rot_dim
    if pad:
        cos = jnp.pad(cos, ((0, 0), (0, pad)))
        sin = jnp.pad(sin, ((0, 0), (0, pad)))
    return cos, sin


def _in_proj_perm():
    o_q = 0
    o_k = o_q + GQA_WIDTH
    o_v = o_k + GQA_KV_WIDTH
    o_cq = o_v + GQA_KV_WIDTH
    o_ckv = o_cq + MLA_Q_LORA
    o_kpe = o_ckv + MLA_KV_LORA
    o_z = o_kpe + MLA_ROPE_DIM
    o_xs = o_z + SSD_INNER
    o_bm = o_xs + SSD_INNER
    o_cm = o_bm + SSD_GROUPS * SSD_STATE
    o_dt = o_cm + SSD_GROUPS * SSD_STATE
    ar = lambda a, n: list(range(a, a + n))
    perm = []
    for g in range(SSD_GROUPS):
        perm += ar(o_xs + g * SSD_GINNER, SSD_GINNER) + ar(o_bm + g * SSD_STATE, SSD_STATE)
        perm += ar(o_cm + g * SSD_STATE, SSD_STATE)
    perm += ar(o_z, SSD_INNER)
    for g in range(SSD_GROUPS):
        perm += ar(o_dt + g * SSD_GHEADS, SSD_GHEADS) + ar(o_dt + SSD_HEADS + g * SSD_GHEADS, SSD_GHEADS)
        perm += [-1] * (LANES - 2 * SSD_GHEADS)
    perm += ar(o_q, GQA_WIDTH) + ar(o_k, GQA_KV_WIDTH) + ar(o_v, GQA_KV_WIDTH)
    perm += ar(o_cq, MLA_Q_LORA) + ar(o_ckv, MLA_KV_LORA)
    perm += ar(o_kpe, MLA_ROPE_DIM) + [-1] * (LANES - MLA_ROPE_DIM)
    assert len(perm) == IN_COLS_PAD
    return np.asarray(perm, np.int32), o_dt + 2 * SSD_HEADS


def _take_cols(w, perm):
    parts = []
    i = 0
    while i < len(perm):
        j = i + 1
        if perm[i] < 0:
            while j < len(perm) and perm[j] < 0:
                j += 1
            parts.append(jnp.zeros(w.shape[:-1] + (j - i,), w.dtype))
        else:
            while j < len(perm) and perm[j] == perm[j - 1] + 1:
                j += 1
            parts.append(w[..., int(perm[i]):int(perm[i]) + (j - i)])
        i = j
    return jnp.concatenate(parts, axis=-1)


def _group_heads(a):
    depth = a.shape[0]
    a = a.reshape(depth, 2, SSD_GROUPS, SSD_GHEADS).transpose(0, 2, 1, 3).reshape(depth, SSD_GROUPS, 2 * SSD_GHEADS)
    return jnp.pad(a, ((0, 0), (0, 0), (0, LANES - 2 * SSD_GHEADS)))[:, :, None, :]


def kernel(x, c, w_ada, b_ada, norm1_g, norm2_g, w_in, q_norm_g, k_norm_g, mla_q_norm_g, w_uq, mla_kv_norm_g,
           w_ukv, conv_w, conv_b, dt_bias, a_log, d_skip, ssd_norm_g, w_out, w_gate_up, w_down, final_norm_g):
    b, s, d = x.shape
    depth = w_in.shape[0]
    tm = min(512, s)
    tq = min(512, s)
    tm_ffn = min(512, s)
    th = 512

    perm, n_in = _in_proj_perm()
    assert n_in == w_in.shape[-1]
    win_p = _take_cols(w_in, perm).astype(BF16)
    uq_perm = np.concatenate([np.concatenate([np.arange(h * MLA_QK_DIM, (h + 1) * MLA_QK_DIM),
                                              -np.ones(MLA_QK_PAD - MLA_QK_DIM, np.int64)])
                              for h in range(MLA_HEADS)]).astype(np.int32)
    wuq_p = _take_cols(w_uq, uq_perm).astype(BF16)
    hw = MLA_NOPE_DIM + MLA_V_DIM
    ukv_perm = np.concatenate([np.arange(h * hw, h * hw + MLA_NOPE_DIM) for h in range(MLA_HEADS)]
                              + [np.arange(h * hw + MLA_NOPE_DIM, (h + 1) * hw) for h in range(MLA_HEADS)])
    wukv_p = _take_cols(w_ukv, ukv_perm).astype(BF16)
    conv_perm = np.concatenate([np.concatenate([np.arange(g * SSD_GINNER, (g + 1) * SSD_GINNER),
                                                SSD_INNER + np.arange(g * SSD_STATE, (g + 1) * SSD_STATE),
                                                SSD_INNER + SSD_GROUPS * SSD_STATE
                                                + np.arange(g * SSD_STATE, (g + 1) * SSD_STATE)])
                                for g in range(SSD_GROUPS)]).astype(np.int32)
    cw_p = _take_cols(conv_w, conv_perm).reshape(depth, SSD_CONV, SSD_GROUPS, SSD_GCONV)
    cw_p = cw_p.transpose(0, 2, 1, 3)
    cb_p = _take_cols(conv_b, conv_perm).reshape(depth, SSD_GROUPS, 1, SSD_GCONV)
    dtb_p = _group_heads(dt_bias)
    alogr_p = _group_heads(a_log)
    alogc_p = jnp.broadcast_to(alogr_p[:, :, 0, :SSD_HEAD_ROWS, None], (depth, SSD_GROUPS, SSD_HEAD_ROWS, LANES))
    dskip_p = jnp.broadcast_to(jnp.repeat(d_skip, SSD_HEAD_DIM, axis=-1).reshape(depth, SSD_GROUPS, SSD_GINNER, 1),
                               (depth, SSD_GROUPS, SSD_GINNER, LANES))
    ng_p = ssd_norm_g.reshape(depth, SSD_GROUPS, 1, SSD_GINNER)
    wout_b = w_out.astype(BF16)
    wgu_b = w_gate_up.astype(BF16)
    wd_b = w_down.astype(BF16)
    cosa, sina = _rope_tables(s, GQA_HEAD_DIM)
    cosb, sinb = _rope_tables(s, MLA_ROPE_DIM)

    mod = _modulation(c, w_ada, b_ada).reshape(depth, b, 6, d)
    fg = final_norm_g.reshape(1, d)

    vec = lambda a: a.reshape(depth, 1, -1)
    for l in range(depth):
        xbc, z, dt, qa, ka, va, qb, kb, vb = _prep(
            x, mod, vec(norm1_g), win_p, vec(q_norm_g), vec(k_norm_g), cosa, sina,
            vec(mla_q_norm_g), wuq_p, vec(mla_kv_norm_g), wukv_p, cosb, sinb, l, tm)
        oa = _attention(qa, ka, va, groups=GQA_KV_HEADS, heads=GQA_HEADS // GQA_KV_HEADS, shared_kv=True,
                        dk=GQA_HEAD_DIM, dv=GQA_HEAD_DIM, tq=tq)
        ob = _attention(qb, kb, vb, groups=MLA_HEADS // 2, heads=2, shared_kv=False,
                        dk=MLA_QK_PAD, dv=MLA_V_DIM, tq=tq)
        oc = _ssd(xbc, z, dt, cw_p[l], cb_p[l], dtb_p[l], alogr_p[l], alogc_p[l], dskip_p[l], ng_p[l])
        x, h2 = _outproj(x, mod, oa, ob, oc, wout_b, vec(norm2_g), l, tm)
        x = _ffn(x, h2, mod, wgu_b, wd_b, fg, l, tm_ffn, th, final=(l == depth - 1))
    return x
```

```python
import functools

import numpy as np
import jax
import jax.numpy as jnp
from jax import lax
from jax.experimental import pallas as pl
from jax.experimental.pallas import tpu as pltpu

F32 = jnp.float32
BF16 = jnp.bfloat16

GRID_W = 64
ROPE_THETA = 10000.0
EPS = 1e-6
LOG2E = 1.4426950408889634

GQA_HEADS = 6
GQA_KV_HEADS = 2
GQA_HEAD_DIM = 128
GQA_WIDTH = GQA_HEADS * GQA_HEAD_DIM
GQA_KV_WIDTH = GQA_KV_HEADS * GQA_HEAD_DIM

MLA_HEADS = 4
MLA_Q_LORA = 512
MLA_KV_LORA = 256
MLA_NOPE_DIM = 128
MLA_ROPE_DIM = 64
MLA_V_DIM = 128
MLA_QK_DIM = MLA_NOPE_DIM + MLA_ROPE_DIM
MLA_QK_PAD = 256
MLA_WIDTH = MLA_HEADS * MLA_V_DIM

SSD_HEADS = 12
SSD_HEAD_DIM = 64
SSD_GROUPS = 2
SSD_STATE = 128
SSD_CONV = 5
SSD_CHUNK = 128
SSD_INNER = SSD_HEADS * SSD_HEAD_DIM
SSD_CONV_DIM = SSD_INNER + 2 * SSD_GROUPS * SSD_STATE
SSD_GHEADS = SSD_HEADS // SSD_GROUPS
SSD_GINNER = SSD_INNER // SSD_GROUPS
SSD_GCONV = SSD_GINNER + 2 * SSD_STATE
SSD_HEAD_ROWS = 16

LANES = 128
CONV_PAD_ROWS = 16
ATTN_KEY_CHUNK = 512

SEG_XBC = (0, SSD_GROUPS * SSD_GCONV)
SEG_Z = (SEG_XBC[1], SEG_XBC[1] + SSD_INNER)
SEG_DT = (SEG_Z[1], SEG_Z[1] + SSD_GROUPS * LANES)
SEG_QA = (SEG_DT[1], SEG_DT[1] + GQA_WIDTH)
SEG_KA = (SEG_QA[1], SEG_QA[1] + GQA_KV_WIDTH)
SEG_VA = (SEG_KA[1], SEG_KA[1] + GQA_KV_WIDTH)
SEG_CQ = (SEG_VA[1], SEG_VA[1] + MLA_Q_LORA)
SEG_CKV = (SEG_CQ[1], SEG_CQ[1] + MLA_KV_LORA)
SEG_KPE = (SEG_CKV[1], SEG_CKV[1] + LANES)
IN_COLS_PAD = SEG_KPE[1]

VMEM_LIMIT = 56 * 1024 * 1024


def _cparams(sem):
    return pltpu.CompilerParams(dimension_semantics=sem, vmem_limit_bytes=VMEM_LIMIT)


def _dot(a, b):
    return jnp.dot(a, b, preferred_element_type=F32)


def _dot_nt(a, b):
    return lax.dot_general(a, b, (((1,), (1,)), ((), ())), preferred_element_type=F32)


def _rms(x, g):
    return x * lax.rsqrt(jnp.mean(x * x, axis=-1, keepdims=True) + EPS) * g


def _silu(x):
    return x * (1.0 / (1.0 + jnp.exp(-x)))


def _swap_quarters(x, q):
    lane = lax.broadcasted_iota(jnp.int32, x.shape, 1)
    even = (lane // q) % 2 == 0
    return jnp.where(even, pltpu.roll(x, LANES - q, 1), pltpu.roll(x, q, 1))


def _rope(x, cos, sin, q):
    return x * cos + _swap_quarters(x, q) * sin


def _mod_kernel(c_ref, w_ref, b_ref, o_ref):
    c = c_ref[...]
    act = _silu(c).astype(BF16)
    o_ref[...] = _dot(act, w_ref[...].astype(BF16)) + b_ref[...]


def _modulation(c, w_ada, b_ada):
    depth, d, n = w_ada.shape
    b = c.shape[0]
    tn = 1024
    return pl.pallas_call(
        _mod_kernel,
        grid=(depth, n // tn),
        in_specs=[
            pl.BlockSpec((b, d), lambda l, j: (0, 0)),
            pl.BlockSpec((None, d, tn), lambda l, j: (l, 0, j)),
            pl.BlockSpec((None, 1, tn), lambda l, j: (l, 0, j)),
        ],
        out_specs=pl.BlockSpec((None, b, tn), lambda l, j: (l, 0, j)),
        out_shape=jax.ShapeDtypeStruct((depth, b, n), F32),
        compiler_params=_cparams(("arbitrary", "arbitrary")),
        name="modulation",
    )(c, w_ada, b_ada.reshape(depth, 1, n))


def _prep_kernel(x_ref, mod_ref, g1_ref, win_ref, qg_ref, kg_ref, cosa_ref, sina_ref,
                 mqg_ref, wuq_ref, mkvg_ref, wukv_ref, cosb_ref, sinb_ref,
                 xbc_ref, z_ref, dt_ref, qat_ref, ka_ref, vat_ref, qbt_ref, kb_ref, vbt_ref):
    x = x_ref[...]
    h = (_rms(x, g1_ref[...]) * (1.0 + mod_ref[1:2, :]) + mod_ref[0:1, :]).astype(BF16)

    def proj(seg):
        return _dot(h, win_ref[:, seg[0]:seg[1]])

    def put_t(ref, blk, val):
        ref[blk * LANES:(blk + 1) * LANES, :] = val.T.astype(BF16)

    cosa = cosa_ref[...]
    sina = sina_ref[...]
    cosb = cosb_ref[...]
    sinb = sinb_ref[...]
    qscale = GQA_HEAD_DIM ** -0.5 * LOG2E
    bscale = MLA_QK_DIM ** -0.5 * LOG2E

    qa = proj(SEG_QA)
    ka = proj(SEG_KA)
    for hd in range(GQA_HEADS):
        sl = slice(hd * GQA_HEAD_DIM, (hd + 1) * GQA_HEAD_DIM)
        q = _rope(_rms(qa[:, sl], qg_ref[...]), cosa, sina, GQA_HEAD_DIM // 4)
        put_t(qat_ref, hd, q * qscale)
    cq_raw = proj(SEG_CQ)
    for hd in range(GQA_KV_HEADS):
        sl = slice(hd * GQA_HEAD_DIM, (hd + 1) * GQA_HEAD_DIM)
        k = _rope(_rms(ka[:, sl], kg_ref[...]), cosa, sina, GQA_HEAD_DIM // 4)
        ka_ref[:, sl] = k.astype(BF16)
    xbc = proj(SEG_XBC)
    cq = _rms(cq_raw, mqg_ref[...]).astype(BF16)
    qb = _dot(cq, wuq_ref[...])
    xbc_ref[...] = xbc.astype(BF16)
    ckv_raw = proj(SEG_CKV)
    kpe_raw = proj(SEG_KPE)
    for hd in range(MLA_HEADS):
        c0 = hd * MLA_QK_PAD
        put_t(qbt_ref, 2 * hd, qb[:, c0:c0 + LANES] * bscale)
        qpe = _rope(qb[:, c0 + LANES:c0 + 2 * LANES], cosb, sinb, MLA_ROPE_DIM // 4)
        put_t(qbt_ref, 2 * hd + 1, qpe * bscale)
    z = proj(SEG_Z)
    ckv = _rms(ckv_raw, mkvg_ref[...]).astype(BF16)
    kv = _dot(ckv, wukv_ref[...])
    z_ref[...] = z.astype(BF16)
    kpe = _rope(kpe_raw, cosb, sinb, MLA_ROPE_DIM // 4).astype(BF16)
    va = proj(SEG_VA)
    dt = proj(SEG_DT)
    for hd in range(MLA_HEADS):
        c0 = hd * MLA_QK_PAD
        kb_ref[:, c0:c0 + LANES] = kv[:, hd * LANES:(hd + 1) * LANES].astype(BF16)
        kb_ref[:, c0 + LANES:c0 + 2 * LANES] = kpe
        put_t(vbt_ref, hd, kv[:, (MLA_HEADS + hd) * LANES:(MLA_HEADS + hd + 1) * LANES])
    for hd in range(GQA_KV_HEADS):
        put_t(vat_ref, hd, va[:, hd * GQA_HEAD_DIM:(hd + 1) * GQA_HEAD_DIM])
    dt_ref[...] = dt


def _layer_spec(a, l, single=False):
    nd = a.ndim - 1
    return pl.BlockSpec((None,) + a.shape[1:], lambda *_: (l,) + (0,) * nd,
                        pipeline_mode=pl.Buffered(1) if single else None)


def _mod_spec(mod, l):
    return pl.BlockSpec((None, None) + mod.shape[2:], lambda bi, *_: (l, bi, 0, 0))


def _prep(x, mod, g1, win, qg, kg, cosa, sina, mqg, wuq, mkvg, wukv, cosb, sinb, l, tm):
    b, s, d = x.shape
    row = lambda w: pl.BlockSpec((None, tm, w), lambda bi, i: (bi, i, 0))
    tab = pl.BlockSpec((tm, LANES), lambda bi, i: (i, 0))
    col = lambda w: pl.BlockSpec((None, w, tm), lambda bi, i: (bi, 0, i))
    outs = ((SSD_CONV_DIM, BF16, False), (SSD_INNER, BF16, False), (SSD_GROUPS * LANES, F32, False),
            (GQA_WIDTH, BF16, True), (GQA_KV_WIDTH, BF16, False), (GQA_KV_WIDTH, BF16, True),
            (MLA_HEADS * MLA_QK_PAD, BF16, True), (MLA_HEADS * MLA_QK_PAD, BF16, False), (MLA_WIDTH, BF16, True))
    big = lambda a: _layer_spec(a, l, single=True)
    small = lambda a: _layer_spec(a, l)
    return pl.pallas_call(
        _prep_kernel,
        grid=(b, s // tm),
        in_specs=[
            row(d), _mod_spec(mod, l),
            small(g1), big(win), small(qg), small(kg), tab, tab,
            small(mqg), big(wuq), small(mkvg), big(wukv), tab, tab,
        ],
        out_specs=[col(w) if t else row(w) for w, _, t in outs],
        out_shape=[jax.ShapeDtypeStruct((b, w, s) if t else (b, s, w), dt) for w, dt, t in outs],
        compiler_params=_cparams(("arbitrary", "arbitrary")),
        name="prep",
    )(x, mod, g1, win, qg, kg, cosa, sina, mqg, wuq, mkvg, wukv, cosb, sinb)


def _attn_kernel(qt_ref, k_ref, vt_ref, o_ref, *, heads, shared_kv, dk, dv):
    s_len = k_ref.shape[0]
    kc = min(ATTN_KEY_CHUNK, s_len)
    nchunk = s_len // kc
    stages = [(r, c) for r in range(heads) for c in range(nchunk)]

    def scores(r, c):
        kv = 0 if shared_kv else r
        return _dot(k_ref[c * kc:(c + 1) * kc, kv * dk:(kv + 1) * dk], qt_ref[r * dk:(r + 1) * dk, :])

    st_next = scores(*stages[0])
    for i, (r, c) in enumerate(stages):
        st = st_next
        if i + 1 < len(stages):
            st_next = scores(*stages[i + 1])
        kv = 0 if shared_kv else r
        vt = vt_ref[kv * dv:(kv + 1) * dv, c * kc:(c + 1) * kc]
        mc = jnp.max(st, axis=0, keepdims=True)
        if c == 0:
            m = mc
            p = jnp.exp2(st - m)
            l = jnp.sum(p, axis=0, keepdims=True)
            acc = _dot(vt, p.astype(BF16))
        else:
            m_new = jnp.maximum(m, mc)
            alpha = jnp.exp2(m - m_new)
            p = jnp.exp2(st - m_new)
            l = alpha * l + jnp.sum(p, axis=0, keepdims=True)
            acc = alpha * acc + _dot(vt, p.astype(BF16))
            m = m_new
        if c == nchunk - 1:
            o_ref[:, r * dv:(r + 1) * dv] = (acc * (1.0 / l)).T.astype(o_ref.dtype)


def _attention(qt, k, vt, *, groups, heads, shared_kv, dk, dv, tq):
    b, _, s = qt.shape
    nkv = 1 if shared_kv else heads
    return pl.pallas_call(
        functools.partial(_attn_kernel, heads=heads, shared_kv=shared_kv, dk=dk, dv=dv),
        grid=(b, groups, s // tq),
        in_specs=[
            pl.BlockSpec((None, heads * dk, tq), lambda bi, g, i: (bi, g, i)),
            pl.BlockSpec((None, s, nkv * dk), lambda bi, g, i: (bi, 0, g)),
            pl.BlockSpec((None, nkv * dv, s), lambda bi, g, i: (bi, g, 0)),
        ],
        out_specs=pl.BlockSpec((None, tq, heads * dv), lambda bi, g, i: (bi, i, g)),
        out_shape=jax.ShapeDtypeStruct((b, s, groups * heads * dv), BF16),
        compiler_params=_cparams(("arbitrary", "arbitrary", "arbitrary")),
        name="attention",
    )(qt, k, vt)


def _split3(a):
    hi = a.astype(BF16)
    r = a - hi.astype(F32)
    mid = r.astype(BF16)
    lo = (r - mid.astype(F32)).astype(BF16)
    return hi, mid, lo


def _head_rows(a, lo_row, width):
    return jnp.concatenate([jnp.broadcast_to(a[lo_row + h:lo_row + h + 1, :], (SSD_HEAD_DIM, width))
                            for h in range(SSD_GHEADS)], axis=0)


def _ssd_kernel(xbc_ref, z_ref, dt_ref, cw_ref, cb_ref, dtb_ref, alogr_ref, alogc_ref, dskip_ref, ng_ref, o_ref,
                xpad_ref, xst_ref, bm_ref, cm_ref, posc_ref, dtt_ref, cst_ref, yt_ref, hf_ref, hb_ref):
    s = xbc_ref.shape[0]
    nc = s // SSD_CHUNK
    L = SSD_CHUNK
    G = SSD_GINNER
    HR = SSD_HEAD_ROWS

    row = lax.broadcasted_iota(jnp.int32, (L, L), 0)
    col = lax.broadcasted_iota(jnp.int32, (L, L), 1)
    s_le_t = row <= col
    s_ge_t = row >= col
    tri = s_ge_t.astype(BF16)
    is_fwd_lane = lax.broadcasted_iota(jnp.int32, (L, LANES), 1) < SSD_GHEADS
    is_fwd_row = lax.broadcasted_iota(jnp.int32, (HR, L), 0) < SSD_GHEADS
    dtb = dtb_ref[...]
    aneg_row = -jnp.exp(alogr_ref[...])
    aneg_col = -jnp.exp(alogc_ref[...])

    zeros_pad = jnp.zeros((CONV_PAD_ROWS, SSD_GCONV), BF16)
    xpad_ref[0:CONV_PAD_ROWS, :] = zeros_pad
    xpad_ref[CONV_PAD_ROWS + s:2 * CONV_PAD_ROWS + s, :] = zeros_pad
    xpad_ref[CONV_PAD_ROWS:CONV_PAD_ROWS + s, :] = xbc_ref[...]
    cw = cw_ref[...]
    cb = cb_ref[...]
    half = SSD_CONV // 2
    wrow = lax.broadcasted_iota(jnp.int32, (L, L + 2 * CONV_PAD_ROWS), 0)
    wcol = lax.broadcasted_iota(jnp.int32, (L, L + 2 * CONV_PAD_ROWS), 1)
    shifts = [(wcol == wrow + (CONV_PAD_ROWS - half + j)).astype(BF16) for j in range(SSD_CONV)]

    def conv_body(c, carry):
        base = pl.multiple_of(c * L, L)
        win = xpad_ref[pl.ds(base, L + 2 * CONV_PAD_ROWS), :]
        acc = cb
        for j in range(SSD_CONV):
            if j == half:
                tap = xbc_ref[pl.ds(base, L), :].astype(F32)
            else:
                tap = _dot(shifts[j], win)
            acc = acc + tap * cw[j:j + 1, :]
        act = _silu(acc)
        for k in range(G // LANES):
            xst_ref[c, k * LANES:(k + 1) * LANES, :] = act[:, k * LANES:(k + 1) * LANES].T
        bm_ref[c] = act[:, G:G + SSD_STATE].astype(BF16)
        cm_ref[c] = act[:, G + SSD_STATE:].astype(BF16)
        dtr = dt_ref[pl.ds(base, L), :] + dtb
        dtv = jnp.maximum(dtr, 0.0) + jnp.log1p(jnp.exp(-jnp.abs(dtr)))
        a = dtv * aneg_row
        hi, mid, lo = _split3(a)
        cs = _dot(tri, hi) + _dot(tri, mid) + _dot(tri, lo)
        posc_ref[c] = jnp.where(is_fwd_lane, cs, cs - a)
        dtt_ref[c] = dtv.T[0:HR, :]
        cst_ref[c] = cs.T[0:HR, :]
        return carry

    lax.fori_loop(0, nc, conv_body, 0, unroll=4)

    dskip = dskip_ref[...]
    hf_ref[...] = jnp.zeros_like(hf_ref)
    hb_ref[...] = jnp.zeros_like(hb_ref)
    neg_big = -1e30
    zero_half = jnp.zeros((SSD_HEAD_DIM, L), F32)

    def direction(ci, h_ref, lo, fwd):
        xst = xst_ref[ci]
        bmc = bm_ref[ci]
        cmc = cm_ref[ci]
        posc = posc_ref[ci]
        dtt = dtt_ref[ci]
        cst = cst_ref[ci]
        tot = jnp.broadcast_to(cst[:, L - 1:L], (HR, L))
        ex = cst - dtt * aneg_col
        post = jnp.where(is_fwd_row, cst, ex)
        d_in = jnp.exp(jnp.where(is_fwd_row, cst, tot - ex))
        d_out = jnp.exp(jnp.where(is_fwd_row, tot - cst, ex))
        cdec = jnp.exp(tot)
        cbt = _dot_nt(bmc, cmc)
        ys = []
        for p in range(SSD_GHEADS // 2):
            ws = []
            lhs = []
            for hh in range(2):
                hr = lo + 2 * p + hh
                pc = posc[:, hr:hr + 1]
                pr = post[hr:hr + 1, :]
                if fwd:
                    seg = jnp.where(s_le_t, pr - pc, neg_big)
                else:
                    seg = jnp.where(s_ge_t, pc - pr, neg_big)
                ws.append((cbt * jnp.exp(seg)).astype(BF16))
                dt_rows = jnp.broadcast_to(dtt[hr:hr + 1, :], (SSD_HEAD_DIM, L))
                sel = [zero_half, zero_half]
                sel[hh] = dt_rows
                lhs.append(xst[p * LANES:(p + 1) * LANES, :] * jnp.concatenate(sel, axis=0))
            ys.append(_dot(jnp.concatenate(lhs, axis=1).astype(BF16), jnp.concatenate(ws, axis=0)))
        y = jnp.concatenate(ys, axis=0)
        hprev = h_ref[...]
        y = y + _dot_nt(hprev.astype(BF16), cmc) * _head_rows(d_in, lo, L)
        xo = (xst * _head_rows(dtt * d_out, lo, L)).astype(BF16)
        h_ref[...] = hprev * _head_rows(cdec, lo, SSD_STATE) + _dot(xo, bmc)
        return y, xst

    def scan_body(c, carry):
        yf, xst = direction(c, hf_ref, 0, True)
        yt_ref[c] = yt_ref[c] + yf + xst * dskip
        cb_ = nc - 1 - c
        yb, _ = direction(cb_, hb_ref, SSD_GHEADS, False)
        yt_ref[cb_] = yt_ref[cb_] + yb
        return carry

    yt_ref[...] = jnp.zeros_like(yt_ref)
    lax.fori_loop(0, nc, scan_body, 0, unroll=4)

    ng = ng_ref[...]

    def out_body(c, carry):
        base = pl.multiple_of(c * L, L)
        yt = yt_ref[c]
        y = jnp.concatenate([yt[k * LANES:(k + 1) * LANES, :].T for k in range(G // LANES)], axis=1)
        y = y * _silu(z_ref[pl.ds(base, L), :].astype(F32))
        o_ref[pl.ds(base, L), :] = _rms(y, ng).astype(o_ref.dtype)
        return carry

    lax.fori_loop(0, nc, out_body, 0, unroll=4)


def _ssd(xbc, z, dt, cw, cb, dtb, alogr, alogc, dskip, ng):
    b, s, _ = xbc.shape
    nc = s // SSD_CHUNK
    par = lambda a: pl.BlockSpec((None,) + a.shape[1:], lambda bi, g: (g, 0, 0))
    return pl.pallas_call(
        _ssd_kernel,
        grid=(b, SSD_GROUPS),
        in_specs=[
            pl.BlockSpec((None, s, SSD_GCONV), lambda bi, g: (bi, 0, g)),
            pl.BlockSpec((None, s, SSD_GINNER), lambda bi, g: (bi, 0, g)),
            pl.BlockSpec((None, s, LANES), lambda bi, g: (bi, 0, g)),
            par(cw), par(cb), par(dtb), par(alogr), par(alogc), par(dskip), par(ng),
        ],
        out_specs=pl.BlockSpec((None, s, SSD_GINNER), lambda bi, g: (bi, 0, g)),
        out_shape=jax.ShapeDtypeStruct((b, s, SSD_INNER), BF16),
        scratch_shapes=[
            pltpu.VMEM((s + 2 * CONV_PAD_ROWS, SSD_GCONV), BF16),
            pltpu.VMEM((nc, SSD_GINNER, SSD_CHUNK), F32),
            pltpu.VMEM((nc, SSD_CHUNK, SSD_STATE), BF16),
            pltpu.VMEM((nc, SSD_CHUNK, SSD_STATE), BF16),
            pltpu.VMEM((nc, SSD_CHUNK, LANES), F32),
            pltpu.VMEM((nc, SSD_HEAD_ROWS, SSD_CHUNK), F32),
            pltpu.VMEM((nc, SSD_HEAD_ROWS, SSD_CHUNK), F32),
            pltpu.VMEM((nc, SSD_GINNER, SSD_CHUNK), F32),
            pltpu.VMEM((SSD_GINNER, SSD_STATE), F32),
            pltpu.VMEM((SSD_GINNER, SSD_STATE), F32),
        ],
        compiler_params=_cparams(("arbitrary", "arbitrary")),
        name="ssd",
    )(xbc, z, dt, cw, cb, dtb, alogr, alogc, dskip, ng)


def _outproj_kernel(x_ref, mod_ref, oa_ref, ob_ref, oc_ref, w_ref, g2_ref, o_ref, h_ref):
    a1 = GQA_WIDTH
    a2 = GQA_WIDTH + MLA_WIDTH
    half = x_ref.shape[0] // 2
    halves = (slice(0, half), slice(half, 2 * half))
    mixes = []
    for rows in halves:
        mix = _dot(oa_ref[rows, :], w_ref[0:a1, :])
        mix = mix + _dot(ob_ref[rows, :], w_ref[a1:a2, :])
        mixes.append(mix + _dot(oc_ref[rows, :], w_ref[a2:, :]))
    for rows, mix in zip(halves, mixes):
        x1 = x_ref[rows, :] + mod_ref[2:3, :] * mix
        o_ref[rows, :] = x1
        h_ref[rows, :] = (_rms(x1, g2_ref[...]) * (1.0 + mod_ref[4:5, :]) + mod_ref[3:4, :]).astype(BF16)


def _outproj(x, mod, oa, ob, oc, w, g2, l, tm):
    b, s, d = x.shape
    row = lambda wd: pl.BlockSpec((None, tm, wd), lambda bi, i: (bi, i, 0))
    return pl.pallas_call(
        _outproj_kernel,
        grid=(b, s // tm),
        in_specs=[row(d), _mod_spec(mod, l), row(oa.shape[-1]), row(ob.shape[-1]), row(oc.shape[-1]),
                  _layer_spec(w, l, single=True), _layer_spec(g2, l)],
        out_specs=[row(d), row(d)],
        out_shape=[jax.ShapeDtypeStruct(x.shape, F32), jax.ShapeDtypeStruct(x.shape, BF16)],
        compiler_params=_cparams(("arbitrary", "arbitrary")),
        name="outproj",
    )(x, mod, oa, ob, oc, w, g2)


def _ffn_kernel(x_ref, h_ref, mod_ref, wg_ref, wu_ref, wd_ref, fg_ref, o_ref, *, final):
    j = pl.program_id(2)

    @pl.when(j == 0)
    def _():
        o_ref[...] = x_ref[...]

    h = h_ref[...]
    part = _dot((_silu(_dot(h, wg_ref[...])) * _dot(h, wu_ref[...])).astype(BF16), wd_ref[...])
    o_ref[...] += mod_ref[5:6, :] * part

    if final:
        @pl.when(j == pl.num_programs(2) - 1)
        def _():
            o_ref[...] = _rms(o_ref[...], fg_ref[...])


def _ffn(x, h, mod, wgu, wd, fg, l, tm, th, final):
    b, s, d = x.shape
    hidden = wd.shape[1]
    nh = hidden // th
    row = pl.BlockSpec((None, tm, d), lambda bi, i, j: (bi, i, 0))
    return pl.pallas_call(
        functools.partial(_ffn_kernel, final=final),
        grid=(b, s // tm, nh),
        in_specs=[
            row, row, _mod_spec(mod, l),
            pl.BlockSpec((None, d, th), lambda bi, i, j: (l, 0, j)),
            pl.BlockSpec((None, d, th), lambda bi, i, j: (l, 0, nh + j)),
            pl.BlockSpec((None, th, d), lambda bi, i, j: (l, j, 0)),
            pl.BlockSpec((1, d), lambda bi, i, j: (0, 0)),
        ],
        out_specs=row,
        out_shape=jax.ShapeDtypeStruct(x.shape, F32),
        compiler_params=_cparams(("arbitrary", "arbitrary", "arbitrary")),
        name="ffn",
    )(x, h, mod, wgu, wgu, wd, fg)


def _rope_tables(seq_len, rot_dim):
    rows = seq_len // GRID_W
    row_idx = jnp.repeat(jnp.arange(rows), GRID_W).astype(F32)
    col_idx = jnp.tile(jnp.arange(GRID_W), rows).astype(F32)
    axis_dim = rot_dim // 2
    inv_freq = jnp.power(ROPE_THETA, -jnp.arange(0, axis_dim, 2, dtype=F32) / axis_dim)
    ang_r = row_idx[:, None] * inv_freq[None, :]
    ang_c = col_idx[:, None] * inv_freq[None, :]
    cos = jnp.concatenate([jnp.cos(ang_r)] * 2 + [jnp.cos(ang_c)] * 2, axis=-1)
    sin = jnp.concatenate([-jnp.sin(ang_r), jnp.sin(ang_r), -jnp.sin(ang_c), jnp.sin(ang_c)], axis=-1)
    pad = LANES - rot_dim
    if pad:
        cos = jnp.pad(cos, ((0, 0), (0, pad)))
        sin = jnp.pad(sin, ((0, 0), (0, pad)))
    return cos, sin


def _in_proj_perm():
    o_q = 0
    o_k = o_q + GQA_WIDTH
    o_v = o_k + GQA_KV_WIDTH
    o_cq = o_v + GQA_KV_WIDTH
    o_ckv = o_cq + MLA_Q_LORA
    o_kpe = o_ckv + MLA_KV_LORA
    o_z = o_kpe + MLA_ROPE_DIM
    o_xs = o_z + SSD_INNER
    o_bm = o_xs + SSD_INNER
    o_cm = o_bm + SSD_GROUPS * SSD_STATE
    o_dt = o_cm + SSD_GROUPS * SSD_STATE
    ar = lambda a, n: list(range(a, a + n))
    perm = []
    for g in range(SSD_GROUPS):
        perm += ar(o_xs + g * SSD_GINNER, SSD_GINNER) + ar(o_bm + g * SSD_STATE, SSD_STATE)
        perm += ar(o_cm + g * SSD_STATE, SSD_STATE)
    perm += ar(o_z, SSD_INNER)
    for g in range(SSD_GROUPS):
        perm += ar(o_dt + g * SSD_GHEADS, SSD_GHEADS) + ar(o_dt + SSD_HEADS + g * SSD_GHEADS, SSD_GHEADS)
        perm += [-1] * (LANES - 2 * SSD_GHEADS)
    perm += ar(o_q, GQA_WIDTH) + ar(o_k, GQA_KV_WIDTH) + ar(o_v, GQA_KV_WIDTH)
    perm += ar(o_cq, MLA_Q_LORA) + ar(o_ckv, MLA_KV_LORA)
    perm += ar(o_kpe, MLA_ROPE_DIM) + [-1] * (LANES - MLA_ROPE_DIM)
    assert len(perm) == IN_COLS_PAD
    return np.asarray(perm, np.int32), o_dt + 2 * SSD_HEADS


def _take_cols(w, perm):
    parts = []
    i = 0
    while i < len(perm):
        j = i + 1
        if perm[i] < 0:
            while j < len(perm) and perm[j] < 0:
                j += 1
            parts.append(jnp.zeros(w.shape[:-1] + (j - i,), w.dtype))
        else:
            while j < len(perm) and perm[j] == perm[j - 1] + 1:
                j += 1
            parts.append(w[..., int(perm[i]):int(perm[i]) + (j - i)])
        i = j
    return jnp.concatenate(parts, axis=-1)


def _group_heads(a):
    depth = a.shape[0]
    a = a.reshape(depth, 2, SSD_GROUPS, SSD_GHEADS).transpose(0, 2, 1, 3).reshape(depth, SSD_GROUPS, 2 * SSD_GHEADS)
    return jnp.pad(a, ((0, 0), (0, 0), (0, LANES - 2 * SSD_GHEADS)))[:, :, None, :]


def kernel(x, c, w_ada, b_ada, norm1_g, norm2_g, w_in, q_norm_g, k_norm_g, mla_q_norm_g, w_uq, mla_kv_norm_g,
           w_ukv, conv_w, conv_b, dt_bias, a_log, d_skip, ssd_norm_g, w_out, w_gate_up, w_down, final_norm_g):
    b, s, d = x.shape
    depth = w_in.shape[0]
    tm = min(512, s)
    tq = min(512, s)
    tm_ffn = min(512, s)
    th = 512

    perm, n_in = _in_proj_perm()
    assert n_in == w_in.shape[-1]
    win_p = _take_cols(w_in, perm).astype(BF16)
    uq_perm = np.concatenate([np.concatenate([np.arange(h * MLA_QK_DIM, (h + 1) * MLA_QK_DIM),
                                              -np.ones(MLA_QK_PAD - MLA_QK_DIM, np.int64)])
                              for h in range(MLA_HEADS)]).astype(np.int32)
    wuq_p = _take_cols(w_uq, uq_perm).astype(BF16)
    hw = MLA_NOPE_DIM + MLA_V_DIM
    ukv_perm = np.concatenate([np.arange(h * hw, h * hw + MLA_NOPE_DIM) for h in range(MLA_HEADS)]
                              + [np.arange(h * hw + MLA_NOPE_DIM, (h + 1) * hw) for h in range(MLA_HEADS)])
    wukv_p = _take_cols(w_ukv, ukv_perm).astype(BF16)
    conv_perm = np.concatenate([np.concatenate([np.arange(g * SSD_GINNER, (g + 1) * SSD_GINNER),
                                                SSD_INNER + np.arange(g * SSD_STATE, (g + 1) * SSD_STATE),
                                                SSD_INNER + SSD_GROUPS * SSD_STATE
                                                + np.arange(g * SSD_STATE, (g + 1) * SSD_STATE)])
                                for g in range(SSD_GROUPS)]).astype(np.int32)
    cw_p = _take_cols(conv_w, conv_perm).reshape(depth, SSD_CONV, SSD_GROUPS, SSD_GCONV)
    cw_p = cw_p.transpose(0, 2, 1, 3)
    cb_p = _take_cols(conv_b, conv_perm).reshape(depth, SSD_GROUPS, 1, SSD_GCONV)
    dtb_p = _group_heads(dt_bias)
    alogr_p = _group_heads(a_log)
    alogc_p = jnp.broadcast_to(alogr_p[:, :, 0, :SSD_HEAD_ROWS, None], (depth, SSD_GROUPS, SSD_HEAD_ROWS, LANES))
    dskip_p = jnp.broadcast_to(jnp.repeat(d_skip, SSD_HEAD_DIM, axis=-1).reshape(depth, SSD_GROUPS, SSD_GINNER, 1),
                               (depth, SSD_GROUPS, SSD_GINNER, LANES))
    ng_p = ssd_norm_g.reshape(depth, SSD_GROUPS, 1, SSD_GINNER)
    wout_b = w_out.astype(BF16)
    wgu_b = w_gate_up.astype(BF16)
    wd_b = w_down.astype(BF16)
    cosa, sina = _rope_tables(s, GQA_HEAD_DIM)
    cosb, sinb = _rope_tables(s, MLA_ROPE_DIM)

    mod = _modulation(c, w_ada, b_ada).reshape(depth, b, 6, d)
    fg = final_norm_g.reshape(1, d)

    vec = lambda a: a.reshape(depth, 1, -1)
    for l in range(depth):
        xbc, z, dt, qa, ka, va, qb, kb, vb = _prep(
            x, mod, vec(norm1_g), win_p, vec(q_norm_g), vec(k_norm_g), cosa, sina,
            vec(mla_q_norm_g), wuq_p, vec(mla_kv_norm_g), wukv_p, cosb, sinb, l, tm)
        oa = _attention(qa, ka, va, groups=GQA_KV_HEADS, heads=GQA_HEADS // GQA_KV_HEADS, shared_kv=True,
                        dk=GQA_HEAD_DIM, dv=GQA_HEAD_DIM, tq=tq)
        ob = _attention(qb, kb, vb, groups=MLA_HEADS // 2, heads=2, shared_kv=False,
                        dk=MLA_QK_PAD, dv=MLA_V_DIM, tq=tq)
        oc = _ssd(xbc, z, dt, cw_p[l], cb_p[l], dtb_p[l], alogr_p[l], alogc_p[l], dskip_p[l], ng_p[l])
        x, h2 = _outproj(x, mod, oa, ob, oc, wout_b, vec(norm2_g), l, tm)
        x = _ffn(x, h2, mod, wgu_b, wd_b, fg, l, tm_ffn, th, final=(l == depth - 1))
    return x
```
